```python
import jax, jax.numpy as jnp
from jax import lax
import numpy as np

D_MODEL = 1024
BATCH = 1
SEQ = 16384
DEPTH = 2
DEC_BATCH = 128
DEC_SEQ = 8
PAST_LEN = 16384
PAGE_SIZE = 128

MLA_HEADS = 8
MLA_NOPE = 64
MLA_ROPE = 32
MLA_V = 64
MLA_Q_RANK = 512
MLA_KV_RANK = 256
FOX_HEADS = 8
FOX_KV_HEADS = 2
FOX_HEAD_DIM = 64
FOX_GROUP = FOX_HEADS // FOX_KV_HEADS
FORGET_BIAS_INIT = 3.0
CONV_DIM = 512
CONV_WIDTH = 3
N_BRANCH = 3
FFN_DIM = 2816
N_MOD = 9
Q_BLOCK = 128
ROPE_THETA = 10000.0
RMS_EPS = 1e-6

IN_WIDTHS = (MLA_Q_RANK, MLA_KV_RANK, MLA_ROPE,
             FOX_HEADS * FOX_HEAD_DIM, FOX_KV_HEADS * FOX_HEAD_DIM, FOX_KV_HEADS * FOX_HEAD_DIM, FOX_HEADS,
             CONV_DIM, CONV_DIM, CONV_DIM,
             N_BRANCH * D_MODEL)
IN_DIM = sum(IN_WIDTHS)
IN_SPLITS = tuple(sum(IN_WIDTHS[:i + 1]) for i in range(len(IN_WIDTHS) - 1))

kernel_name = 'hybrid_mla_fox_shortconv_macaron_step'


def _rmsnorm(x, g):
    xf = x.astype(jnp.float32)
    y = xf * lax.rsqrt(jnp.mean(jnp.square(xf), axis=-1, keepdims=True) + RMS_EPS)
    return (y * g.astype(jnp.float32)).astype(x.dtype)


def _rope(x, pos):
    half = x.shape[-1] // 2
    inv = ROPE_THETA ** (-jnp.arange(half, dtype=jnp.float32) / half)
    ang = pos.astype(jnp.float32)[:, None] * inv[None, :]
    ang = ang.reshape((1, pos.shape[0]) + (1,) * (x.ndim - 3) + (half,))
    cos, sin = jnp.cos(ang), jnp.sin(ang)
    xf = x.astype(jnp.float32)
    x1, x2 = xf[..., :half], xf[..., half:]
    return jnp.concatenate([x1 * cos - x2 * sin, x1 * sin + x2 * cos], axis=-1).astype(x.dtype)


def _swiglu(h, w_in, w_out):
    a = h @ w_in
    g, u = jnp.split(a, 2, axis=-1)
    return (jax.nn.silu(g) * u) @ w_out


def _sweep_queries(block_fn, q_arrays):
    b, t = q_arrays[0].shape[:2]
    qb = Q_BLOCK if t % Q_BLOCK == 0 else t
    nb = t // qb
    split = lambda a: jnp.moveaxis(a.reshape((b, nb, qb) + a.shape[2:]), 1, 0)
    out = lax.map(block_fn, tuple(split(a) for a in q_arrays))
    return jnp.moveaxis(out, 0, 1).reshape((b, t) + out.shape[3:])


def _masked_softmax(s, mask):
    return jax.nn.softmax(jnp.where(mask, s, -jnp.inf), axis=-1)


def _mla_attention(q_lat, q_rope, qpos, c_all, kr_all, kpos):
    scale = (MLA_NOPE + MLA_ROPE) ** -0.5

    def block(args):
        ql, qr, qp = args
        s = (jnp.einsum('bqhr,bsr->bhqs', ql, c_all, preferred_element_type=jnp.float32)
             + jnp.einsum('bqhd,bsd->bhqs', qr, kr_all, preferred_element_type=jnp.float32)) * scale
        mask = kpos[None, None, None, :] <= qp[:, None, :, None]
        p = _masked_softmax(s, mask).astype(c_all.dtype)
        return jnp.einsum('bhqs,bsr->bqhr', p, c_all)

    return _sweep_queries(block, (q_lat, q_rope, qpos))


def _fox_attention(q, cq, qpos, k_all, v_all, ck, kpos):
    scale = FOX_HEAD_DIM ** -0.5
    ck_t = jnp.transpose(ck, (0, 2, 3, 1))[:, :, :, None, :]

    def block(args):
        qq, cqb, qp = args
        s = jnp.einsum('bqghd,bsgd->bghqs', qq, k_all, preferred_element_type=jnp.float32) * scale
        s = s + jnp.transpose(cqb, (0, 2, 3, 1))[..., None] - ck_t
        mask = kpos[None, None, None, None, :] <= qp[:, None, None, :, None]
        p = _masked_softmax(s, mask).astype(v_all.dtype)
        return jnp.einsum('bghqs,bsgd->bqghd', p, v_all)

    return _sweep_queries(block, (q, cq, qpos))


def _mixer(h, pos, past, l, W):
    b, t, _ = h.shape
    z = h @ W['w_in'][l]
    zq, zkv, zkr, zfq, zfk, zfv, zff, zb, zc, zh, zg = jnp.split(z, IN_SPLITS, axis=-1)

    cq = _rmsnorm(zq, W['g_q'][l])
    q = (cq @ W['w_uq'][l]).reshape(b, t, MLA_HEADS, MLA_NOPE + MLA_ROPE)
    q_nope = q[..., :MLA_NOPE]
    q_rope = _rope(q[..., MLA_NOPE:], pos)
    c_new = _rmsnorm(zkv, W['g_kv'][l])
    kr_new = _rope(zkr, pos)
    q_lat = jnp.einsum('bthn,rhn->bthr', q_nope, W['w_uk'][l])

    fq = zfq.reshape(b, t, FOX_KV_HEADS, FOX_GROUP, FOX_HEAD_DIM)
    fk_new = zfk.reshape(b, t, FOX_KV_HEADS, FOX_HEAD_DIM)
    fv_new = zfv.reshape(b, t, FOX_KV_HEADS, FOX_HEAD_DIM)
    logf_new = jax.nn.log_sigmoid((zff + W['b_forget'][l]).astype(jnp.float32)).astype(h.dtype)

    u = zc * zh

    if past is None:
        c_all, kr_all, fk_all, fv_all, logf_all = c_new, kr_new, fk_new, fv_new, logf_new
        conv_buf = jnp.zeros((b, CONV_WIDTH - 1, CONV_DIM), u.dtype)
    else:
        p_c, p_kr, p_fk, p_fv, p_logf, conv_buf = past
        c_all = jnp.concatenate([p_c, c_new], axis=1)
        kr_all = jnp.concatenate([p_kr, kr_new], axis=1)
        fk_all = jnp.concatenate([p_fk, fk_new], axis=1)
        fv_all = jnp.concatenate([p_fv, fv_new], axis=1)
        logf_all = jnp.concatenate([p_logf, logf_new], axis=1)
    kpos = jnp.arange(c_all.shape[1], dtype=jnp.int32)
    qpos = jnp.broadcast_to(pos[None, :], (b, t))

    o_lat = _mla_attention(q_lat, q_rope, qpos, c_all, kr_all, kpos)
    o_mla = jnp.einsum('bthr,rhv->bthv', o_lat, W['w_uv'][l]).reshape(b, t, MLA_HEADS * MLA_V)

    cum = jnp.cumsum(logf_all.astype(jnp.float32), axis=1)
    ck = cum.reshape(b, cum.shape[1], FOX_KV_HEADS, FOX_GROUP)
    o_fox = _fox_attention(fq, ck[:, -t:], qpos, fk_all, fv_all, ck, kpos)
    o_fox = o_fox.reshape(b, t, FOX_HEADS * FOX_HEAD_DIM)

    w_conv = W['w_conv'][l]
    u_pad = jnp.concatenate([conv_buf, u], axis=1)
    y = w_conv[0] * u_pad[:, 0:t]
    for k in range(1, CONV_WIDTH):
        y = y + w_conv[k] * u_pad[:, k:k + t]
    o_conv = zb * y
    new_conv = u_pad[:, -(CONV_WIDTH - 1):]

    gates = jax.nn.sigmoid((zg + W['b_gate'][l]).astype(jnp.float32)).astype(h.dtype)
    gates = gates.reshape(b, t, N_BRANCH, D_MODEL)
    m = (gates[:, :, 0] * (o_mla @ W['w_pa'][l])
         + gates[:, :, 1] * (o_fox @ W['w_pb'][l])
         + gates[:, :, 2] * (o_conv @ W['w_pc'][l]))
    out = m @ W['w_o'][l]
    return out, (c_new, kr_new, fk_new, fv_new, logf_new, new_conv)


def _layer(x, c, pos, past, l, W):
    b = c.shape[0]
    mod = (jax.nn.silu(c) @ W['w_ada'][l] + W['b_ada'][l]).reshape(b, 1, N_MOD, D_MODEL)
    sh1, sc1, gt1, sh2, sc2, gt2, sh3, sc3, gt3 = [mod[:, :, i] for i in range(N_MOD)]
    h = _rmsnorm(x, W['g_ffn1'][l]) * (1 + sc1) + sh1
    x = x + 0.5 * gt1 * _swiglu(h, W['w_ff1_in'][l], W['w_ff1_out'][l])
    h = _rmsnorm(x, W['g_mix'][l]) * (1 + sc2) + sh2
    mix, state = _mixer(h, pos, past, l, W)
    x = x + gt2 * mix
    h = _rmsnorm(x, W['g_ffn2'][l]) * (1 + sc3) + sh3
    x = x + 0.5 * gt3 * _swiglu(h, W['w_ff2_in'][l], W['w_ff2_out'][l])
    return x, state


def _gather_pages(cache, l, page_table):
    g = cache[l, page_table]
    return g.reshape((g.shape[0], g.shape[1] * g.shape[2]) + g.shape[3:])


def _trunk(x, c, W, caches=None, page_table=None):
    past_len = 0 if caches is None else page_table.shape[1] * caches[0].shape[2]
    pos = past_len + jnp.arange(x.shape[1], dtype=jnp.int32)
    per_layer = []
    for l in range(DEPTH):
        if caches is None:
            past = None
        else:
            lat, kr, fk, fv, lf, conv = caches
            past = (_gather_pages(lat, l, page_table), _gather_pages(kr, l, page_table),
                    _gather_pages(fk, l, page_table), _gather_pages(fv, l, page_table),
                    _gather_pages(lf, l, page_table), conv[l])
        x, st = _layer(x, c, pos, past, l, W)
        per_layer.append(st)
    y = _rmsnorm(x, W['g_final'])
    new_state = tuple(jnp.stack([st[i] for st in per_layer], axis=0) for i in range(len(per_layer[0])))
    return y, new_state


def setup_inputs(seed: int = 0) -> dict:
    key = jax.random.key(seed)
    ks = iter(jax.random.split(key, 48))
    f32 = jnp.float32

    def nrm(shape, scale=1.0):
        return jax.random.normal(next(ks), shape, f32) * scale

    def gain(shape):
        return 1.0 + 0.1 * nrm(shape)

    D, F = D_MODEL, FFN_DIM
    n_pages = PAST_LEN // PAGE_SIZE
    n_used = DEC_BATCH * n_pages
    n_pool = n_used + max(1, n_used // 4)
    perm = jax.random.permutation(next(ks), n_pool)[:n_used]
    page_table = perm.reshape(DEC_BATCH, n_pages).astype(jnp.int32)
    return dict(
        x_prompt=nrm((BATCH, SEQ, D)),
        x_sample=nrm((DEC_BATCH, DEC_SEQ, D)),
        cache_mla_latent=nrm((DEPTH, n_pool, PAGE_SIZE, MLA_KV_RANK)),
        cache_mla_krope=nrm((DEPTH, n_pool, PAGE_SIZE, MLA_ROPE)),
        cache_fox_k=nrm((DEPTH, n_pool, PAGE_SIZE, FOX_KV_HEADS, FOX_HEAD_DIM)),
        cache_fox_v=nrm((DEPTH, n_pool, PAGE_SIZE, FOX_KV_HEADS, FOX_HEAD_DIM)),
        cache_fox_logf=jax.nn.log_sigmoid(FORGET_BIAS_INIT + nrm((DEPTH, n_pool, PAGE_SIZE, FOX_HEADS))),
        state_conv=nrm((DEPTH, DEC_BATCH, CONV_WIDTH - 1, CONV_DIM)),
        page_table=page_table,
        c_prompt=nrm((BATCH, D)),
        c_sample=nrm((DEC_BATCH, D)),
        w_ada=nrm((DEPTH, D, N_MOD * D), D ** -0.5),
        b_ada=nrm((DEPTH, N_MOD * D), 0.02),
        g_ffn1=gain((DEPTH, D)),
        w_ff1_in=nrm((DEPTH, D, 2 * F), D ** -0.5),
        w_ff1_out=nrm((DEPTH, F, D), F ** -0.5),
        g_mix=gain((DEPTH, D)),
        w_in=nrm((DEPTH, D, IN_DIM), D ** -0.5),
        g_q=gain((DEPTH, MLA_Q_RANK)),
        w_uq=nrm((DEPTH, MLA_Q_RANK, MLA_HEADS * (MLA_NOPE + MLA_ROPE)), MLA_Q_RANK ** -0.5),
        g_kv=gain((DEPTH, MLA_KV_RANK)),
        w_uk=nrm((DEPTH, MLA_KV_RANK, MLA_HEADS, MLA_NOPE), MLA_KV_RANK ** -0.5),
        w_uv=nrm((DEPTH, MLA_KV_RANK, MLA_HEADS, MLA_V), MLA_KV_RANK ** -0.5),
        b_forget=FORGET_BIAS_INIT + nrm((DEPTH, FOX_HEADS), 0.1),
        w_conv=nrm((DEPTH, CONV_WIDTH, CONV_DIM), CONV_WIDTH ** -0.5),
        b_gate=nrm((DEPTH, N_BRANCH * D), 0.02),
        w_pa=nrm((DEPTH, MLA_HEADS * MLA_V, D), (MLA_HEADS * MLA_V) ** -0.5),
        w_pb=nrm((DEPTH, FOX_HEADS * FOX_HEAD_DIM, D), (FOX_HEADS * FOX_HEAD_DIM) ** -0.5),
        w_pc=nrm((DEPTH, CONV_DIM, D), CONV_DIM ** -0.5),
        w_o=nrm((DEPTH, D, D), D ** -0.5),
        g_ffn2=gain((DEPTH, D)),
        w_ff2_in=nrm((DEPTH, D, 2 * F), D ** -0.5),
        w_ff2_out=nrm((DEPTH, F, D), F ** -0.5),
        g_final=gain((D,)),
    )


def reference(x_prompt, x_sample, cache_mla_latent, cache_mla_krope, cache_fox_k, cache_fox_v,
              cache_fox_logf, state_conv, page_table, c_prompt, c_sample,
              w_ada, b_ada, g_ffn1, w_ff1_in, w_ff1_out, g_mix, w_in, g_q, w_uq, g_kv, w_uk, w_uv,
              b_forget, w_conv, b_gate, w_pa, w_pb, w_pc, w_o, g_ffn2, w_ff2_in, w_ff2_out, g_final):
    W = dict(w_ada=w_ada, b_ada=b_ada, g_ffn1=g_ffn1, w_ff1_in=w_ff1_in, w_ff1_out=w_ff1_out,
             g_mix=g_mix, w_in=w_in, g_q=g_q, w_uq=w_uq, g_kv=g_kv, w_uk=w_uk, w_uv=w_uv,
             b_forget=b_forget, w_conv=w_conv, b_gate=b_gate, w_pa=w_pa, w_pb=w_pb, w_pc=w_pc,
             w_o=w_o, g_ffn2=g_ffn2, w_ff2_in=w_ff2_in, w_ff2_out=w_ff2_out, g_final=g_final)
    caches = (cache_mla_latent, cache_mla_krope, cache_fox_k, cache_fox_v, cache_fox_logf, state_conv)
    y_prompt, (p_lat, p_kr, p_fk, p_fv, p_lf, p_conv) = _trunk(x_prompt, c_prompt, W)
    y_sample, (s_lat, s_kr, s_fk, s_fv, s_lf, s_conv) = _trunk(x_sample, c_sample, W, caches, page_table)
    return (y_prompt, y_sample, p_lat, p_kr, p_fk, p_fv, p_lf, p_conv,
            s_lat, s_kr, s_fk, s_fv, s_lf, s_conv)
```

```python
import functools

import numpy as np
import jax
import jax.numpy as jnp
from jax import lax
from jax.experimental import pallas as pl
from jax.experimental.pallas import tpu as pltpu

F32 = jnp.float32
BF = jnp.bfloat16

RMS_EPS = 1e-6
ROPE_THETA = 10000.0
LANE = 128
SUBLANE = 8
VMEM_LIMIT = 56 * 1024 * 1024

NT_DIMS = (((1,), (1,)), ((), ()))


def _dot(a, b):
    return jnp.dot(a, b, preferred_element_type=F32)


def _dot_nt(a, b):
    return lax.dot_general(a, b, NT_DIMS, preferred_element_type=F32)


def _split3(x):
    hi = x.astype(BF)
    r1 = x - hi.astype(F32)
    mid = r1.astype(BF)
    lo = (r1 - mid.astype(F32)).astype(BF)
    return hi, mid, lo


def _dot_exact_rhs(a_bf, x):
    hi, mid, lo = _split3(x)
    return _dot(a_bf, hi) + _dot(a_bf, mid) + _dot(a_bf, lo)


def _sigmoid(x):
    return 1.0 / (1.0 + jnp.exp(-x))


def _rms(x, g):
    ms = jnp.mean(x * x, axis=-1, keepdims=True)
    return x * lax.rsqrt(ms + RMS_EPS) * g


def _const_spec(shape):
    nd = len(shape)
    return pl.BlockSpec(shape, lambda *_: (0,) * nd, pipeline_mode=pl.Buffered(1))


def _params(sem):
    return pltpu.CompilerParams(dimension_semantics=sem, vmem_limit_bytes=VMEM_LIMIT)


def _row_blocking(S, R, tm):
    if R >= tm:
        assert R % tm == 0
        return 1, tm
    assert tm % R == 0 and S % (tm // R) == 0
    return tm // R, R


def _x_spec(S, R, D, bs, rb):
    per_seq = R // rb
    return pl.BlockSpec((bs, rb, D), lambda i: (i // per_seq, i % per_seq, 0))


def _mod_spec(D, bs, per_seq, k):
    return pl.BlockSpec((bs, 1, D), lambda i: (i // per_seq, 0, k))


def _rows_spec(width, tm):
    return pl.BlockSpec((tm, width), lambda i: (i, 0))


def _modulated_norm(x3, g_ref, sc_ref, sh_ref):
    bs, rb, D = x3.shape
    h3 = _rms(x3, g_ref[...].reshape(1, 1, D)) * (1.0 + sc_ref[...]) + sh_ref[...]
    return h3.reshape(bs * rb, D).astype(BF)


def _mod_kernel(c_ref, w_ref, b_ref, o_ref):
    c = c_ref[...]
    a = (c * _sigmoid(c)).astype(BF)
    o_ref[0] = _dot(a, w_ref[0]) + b_ref[0]


def _mod_call(c_rows, w_ada, b_ada):
    L, D, M = w_ada.shape
    Rc = c_rows.shape[0]
    tn = D
    return pl.pallas_call(
        _mod_kernel,
        grid=(L, M // tn),
        in_specs=[pl.BlockSpec((Rc, D), lambda l, j: (0, 0)),
                  pl.BlockSpec((1, D, tn), lambda l, j: (l, 0, j)),
                  pl.BlockSpec((1, 1, tn), lambda l, j: (l, 0, j))],
        out_specs=pl.BlockSpec((1, Rc, tn), lambda l, j: (l, 0, j)),
        out_shape=jax.ShapeDtypeStruct((L, Rc, M), F32),
        compiler_params=_params(("parallel", "parallel")),
        name="ada_mod",
    )(c_rows, w_ada, b_ada.reshape(L, 1, M))


def _ffn_kernel(x_ref, sh_ref, sc_ref, gt_ref, g_ref, wg_ref, wu_ref, wo_ref, o_ref, *, n_chunks):
    x3 = x_ref[...]
    bs, rb, D = x3.shape
    h = _modulated_norm(x3, g_ref, sc_ref, sh_ref)
    fc = wg_ref.shape[1] // n_chunks
    acc = jnp.zeros((bs * rb, D), F32)
    for c in range(n_chunks):
        g = _dot(h, wg_ref[:, c * fc:(c + 1) * fc])
        u = _dot(h, wu_ref[:, c * fc:(c + 1) * fc])
        a = (g * _sigmoid(g) * u).astype(BF)
        acc = acc + _dot(a, wo_ref[c * fc:(c + 1) * fc, :])
    o_ref[...] = x3 + 0.5 * gt_ref[...] * acc.reshape(bs, rb, D)


def _ffn_call(x, mod, k0, g, wg, wu, wo, tm):
    S, R, D = x.shape
    F = wg.shape[1]
    bs, rb = _row_blocking(S, R, tm)
    per_seq = R // rb
    n_chunks = 2 if F % (2 * LANE) == 0 else 1
    return pl.pallas_call(
        functools.partial(_ffn_kernel, n_chunks=n_chunks),
        grid=(S * R // tm,),
        in_specs=[_x_spec(S, R, D, bs, rb),
                  _mod_spec(D, bs, per_seq, k0), _mod_spec(D, bs, per_seq, k0 + 1),
                  _mod_spec(D, bs, per_seq, k0 + 2),
                  _const_spec((1, D)), _const_spec((D, F)), _const_spec((D, F)), _const_spec((F, D))],
        out_specs=_x_spec(S, R, D, bs, rb),
        out_shape=jax.ShapeDtypeStruct((S, R, D), F32),
        compiler_params=_params(("parallel",)),
        name="ffn",
    )(x, mod, mod, mod, g, wg, wu, wo)


class _Seg:
    def __init__(self, q_rank, kv_rank, hq, conv):
        widths = dict(zq=q_rank, zkv=kv_rank, misc=3 * LANE, fq=hq * LANE, fkv=2 * LANE,
                      fkp=2 * LANE, fvp=2 * LANE, conv=3 * conv)
        self.off = {}
        o = 0
        for name, w in widths.items():
            self.off[name] = (o, o + w)
            o += w
        self.total = o


def _mix_in_kernel(*refs, seg, heads, prompt, scale):
    it = iter(refs)
    x_ref, sh_ref, sc_ref, gmix_ref, w_ref = (next(it) for _ in range(5))
    gq_ref, wuqa_ref, wuqb_ref, gkv_ref = (next(it) for _ in range(4))
    if prompt:
        wk_ref, wv_ref = next(it), next(it)
    else:
        wukt_ref = next(it)
    bf_ref, wconv_ref, cos_ref, sin_ref, ltri_ref = (next(it) for _ in range(5))
    if prompt:
        pq_ref, pk_ref = next(it), next(it)
    else:
        halo_a_ref, halo_b_ref = next(it), next(it)
    c_out, kr_out, fk_out, fv_out, lf_out, oconv_out, utail_out, qm_out = (next(it) for _ in range(8))
    if prompt:
        km_out, vm_out, qf_out, kf_out, vf_out = (next(it) for _ in range(5))
    else:
        qlat_out, qf_out, cum_out = (next(it) for _ in range(3))
    ubuf = next(it)
    if prompt:
        cum_carry = next(it)

    i = pl.program_id(0)
    x3 = x_ref[...]
    h = _modulated_norm(x3, gmix_ref, sc_ref, sh_ref)
    tm = h.shape[0]

    def proj(name):
        a, b = seg.off[name]
        return _dot(h, w_ref[:, a:b])

    cos = cos_ref[...]
    sin = sin_ref[...]

    cq = _rms(proj("zq"), gq_ref[...]).astype(BF)
    qa = _dot(cq, wuqa_ref[...])
    qb = _dot(cq, wuqb_ref[...])
    q_heads = []
    for hh in range(heads):
        sl = slice(hh * LANE, (hh + 1) * LANE)
        qh = ((qa[:, sl] * cos + qb[:, sl] * sin) * scale).astype(BF)
        qm_out[:, sl] = qh
        q_heads.append(qh)

    c_new = _rms(proj("zkv"), gkv_ref[...])
    c_out[...] = c_new
    cb = c_new.astype(BF)
    zm = proj("misc")
    kr = zm[:, 0:LANE] * cos + zm[:, LANE:2 * LANE] * sin
    kr_out[...] = kr[:, 0:kr_out.shape[1]]
    if prompt:
        knope = _dot(cb, wk_ref[...])
        for hh in range(heads):
            sl = slice(hh * LANE, (hh + 1) * LANE)
            km_out[:, sl] = (knope[:, sl] + kr).astype(BF)
        vm_out[...] = _dot(cb, wv_ref[...]).astype(BF)
    else:
        lat = qlat_out.shape[1] // heads
        for hh in range(heads):
            qlat_out[:, hh * lat:(hh + 1) * lat] = _dot(q_heads[hh], wukt_ref[hh]).astype(BF)

    zff = zm[:, 2 * LANE:3 * LANE] + bf_ref[...]
    logf = jnp.minimum(zff, 0.0) - jnp.log1p(jnp.exp(-jnp.abs(zff)))
    lf_out[...] = logf[:, 0:lf_out.shape[1]]
    zfkv = proj("fkv")
    fk_out[...] = zfkv[:, 0:LANE]
    fv_out[...] = zfkv[:, LANE:2 * LANE]
    cum = _dot_exact_rhs(ltri_ref[...], logf)
    zfq = proj("fq")
    if prompt:
        @pl.when(i == 0)
        def _():
            cum_carry[...] = jnp.zeros_like(cum_carry)
        cum = cum + cum_carry[...]
        cum_carry[...] = cum[tm - 1:tm, :]
        hi, mid, lo = _split3(cum)
        lane = lax.broadcasted_iota(jnp.int32, hi.shape, 1)
        hi = jnp.where(lane == heads, jnp.ones_like(hi), hi)
        sq = jnp.concatenate([hi, mid, lo], axis=1)
        qf_out[...] = (zfq + _dot(sq, pq_ref[...])).astype(BF)
        kf_out[...] = (proj("fkp") + _dot(sq, pk_ref[...])).astype(BF)
        vf_out[...] = proj("fvp").astype(BF)
    else:
        qf_out[...] = zfq.astype(BF)
        cum_out[...] = cum

    zc = proj("conv")
    cw = zc.shape[1] // 3
    zb, u = zc[:, 0:cw], zc[:, cw:2 * cw] * zc[:, 2 * cw:3 * cw]
    @pl.when(i == 0)
    def _():
        ubuf[0:SUBLANE, :] = jnp.zeros((SUBLANE, cw), F32)
    ubuf[SUBLANE:SUBLANE + tm, :] = u
    um1 = ubuf[SUBLANE - 1:SUBLANE - 1 + tm, :]
    um2 = ubuf[SUBLANE - 2:SUBLANE - 2 + tm, :]
    if not prompt:
        r = lax.broadcasted_iota(jnp.int32, u.shape, 0) % SUBLANE
        um2 = jnp.where(r < 2, halo_a_ref[...], um2)
        um1 = jnp.where(r == 0, halo_b_ref[...], um1)
    wc = wconv_ref[...]
    y = wc[0:1, :] * um2 + wc[1:2, :] * um1 + wc[2:3, :] * u
    oconv_out[...] = (zb * y).astype(BF)
    if prompt:
        utail_out[0] = u[tm - SUBLANE:tm, :]
        ubuf[0:SUBLANE, :] = u[tm - SUBLANE:tm, :]
    else:
        utail_out[...] = u


def _mix_in_call(x, mod, wts, tables, prompt, tm, extra):
    S, R, D = x.shape
    N = S * R
    bs, rb = _row_blocking(S, R, tm)
    per_seq = R // rb
    seg, heads = wts["seg"], wts["heads"]
    hq = wts["fox_heads"]
    nblk = N // tm
    kr_w, lf_w, cw = wts["rope"], wts["fox_heads"], wts["conv"]
    in_arrays = [x, mod, mod, wts["g_mix"], wts["w_in"], wts["g_q"], wts["wuq_a"], wts["wuq_b"], wts["g_kv"]]
    in_specs = [_x_spec(S, R, D, bs, rb), _mod_spec(D, bs, per_seq, 3), _mod_spec(D, bs, per_seq, 4),
                _const_spec((1, D)), _const_spec(wts["w_in"].shape), _const_spec(wts["g_q"].shape),
                _const_spec(wts["wuq_a"].shape), _const_spec(wts["wuq_b"].shape), _const_spec(wts["g_kv"].shape)]
    if prompt:
        in_arrays += [wts["wk_pad"], wts["wv_pad"]]
        in_specs += [_const_spec(wts["wk_pad"].shape), _const_spec(wts["wv_pad"].shape)]
    else:
        in_arrays += [wts["wukt_pad"]]
        in_specs += [_const_spec(wts["wukt_pad"].shape)]
    in_arrays += [wts["b_forget"], wts["w_conv"], tables["cos"], tables["sin"], extra["ltri"]]
    in_specs += [_const_spec(wts["b_forget"].shape), _const_spec(wts["w_conv"].shape),
                 _rows_spec(LANE, tm), _rows_spec(LANE, tm), _const_spec((tm, tm))]
    if prompt:
        in_arrays += [extra["pq"], extra["pk"]]
        in_specs += [_const_spec(extra["pq"].shape), _const_spec(extra["pk"].shape)]
    else:
        in_arrays += [extra["halo_a"], extra["halo_b"]]
        in_specs += [_rows_spec(cw, tm), _rows_spec(cw, tm)]

    lat = wts["kv_rank"]
    out = [((N, lat), F32), ((N, kr_w), F32), ((N, LANE), F32), ((N, LANE), F32), ((N, lf_w), F32),
           ((N, cw), BF)]
    out_specs = [_rows_spec(lat, tm), _rows_spec(kr_w, tm), _rows_spec(LANE, tm), _rows_spec(LANE, tm),
                 _rows_spec(lf_w, tm), _rows_spec(cw, tm)]
    if prompt:
        out += [((nblk, SUBLANE, cw), F32)]
        out_specs += [pl.BlockSpec((1, SUBLANE, cw), lambda i: (i, 0, 0))]
    else:
        out += [((N, cw), F32)]
        out_specs += [_rows_spec(cw, tm)]
    out += [((N, heads * LANE), BF)]
    out_specs += [_rows_spec(heads * LANE, tm)]
    if prompt:
        out += [((N, heads * LANE), BF), ((N, heads * LANE), BF), ((N, hq * LANE), BF),
                ((N, 2 * LANE), BF), ((N, 2 * LANE), BF)]
        out_specs += [_rows_spec(heads * LANE, tm), _rows_spec(heads * LANE, tm), _rows_spec(hq * LANE, tm),
                      _rows_spec(2 * LANE, tm), _rows_spec(2 * LANE, tm)]
    else:
        out += [((N, heads * lat), BF), ((N, hq * LANE), BF), ((N, LANE), F32)]
        out_specs += [_rows_spec(heads * lat, tm), _rows_spec(hq * LANE, tm), _rows_spec(LANE, tm)]
    scratch = [pltpu.VMEM((tm + SUBLANE, cw), F32)]
    if prompt:
        scratch += [pltpu.VMEM((1, LANE), F32)]
    return pl.pallas_call(
        functools.partial(_mix_in_kernel, seg=seg, heads=heads, prompt=prompt, scale=wts["mla_scale"]),
        grid=(nblk,),
        in_specs=in_specs,
        out_specs=out_specs,
        out_shape=[jax.ShapeDtypeStruct(s, d) for s, d in out],
        scratch_shapes=scratch,
        compiler_params=_params(("arbitrary",)),
        name="mix_in_prompt" if prompt else "mix_in_sample",
    )(*in_arrays)


def _flash_kernel(q_ref, k_ref, v_ref, o_ref, *, tk):
    i = pl.program_id(1)
    q = q_ref[...]
    tq = q.shape[0]

    def step(j, carry, masked):
        m, l, acc = carry
        start = pl.multiple_of(j * tk, tk)
        k = k_ref[pl.ds(start, tk), :]
        v = v_ref[pl.ds(start, tk), :]
        s = _dot_nt(q, k)
        if masked:
            rows = i * tq + lax.broadcasted_iota(jnp.int32, s.shape, 0)
            cols = j * tk + lax.broadcasted_iota(jnp.int32, s.shape, 1)
            s = jnp.where(cols <= rows, s, -jnp.inf)
        m_new = jnp.maximum(m, jnp.max(s, axis=-1, keepdims=True))
        alpha = jnp.exp(m - m_new)
        p = jnp.exp(s - m_new)
        l = alpha * l + jnp.sum(p, axis=-1, keepdims=True)
        acc = alpha * acc + _dot(p.astype(BF), v)
        return m_new, l, acc

    init = (jnp.full((tq, 1), -jnp.inf, F32), jnp.zeros((tq, 1), F32), jnp.zeros((tq, LANE), F32))
    n_full = i * (tq // tk)
    carry = lax.fori_loop(0, n_full, lambda j, c: step(j, c, False), init)
    for d in range(tq // tk):
        carry = step(n_full + d, carry, True)
    _, l, acc = carry
    o_ref[...] = (acc / l).astype(o_ref.dtype)


def _flash_call(q, k, v, tq, tk):
    N, qw = q.shape
    hq, hk = qw // LANE, k.shape[1] // LANE
    group = hq // hk
    return pl.pallas_call(
        functools.partial(_flash_kernel, tk=tk),
        grid=(hq, N // tq),
        in_specs=[pl.BlockSpec((tq, LANE), lambda h, i: (i, h)),
                  pl.BlockSpec((N, LANE), lambda h, i: (0, h // group)),
                  pl.BlockSpec((N, LANE), lambda h, i: (0, h // group))],
        out_specs=pl.BlockSpec((tq, LANE), lambda h, i: (i, h)),
        out_shape=jax.ShapeDtypeStruct((N, qw), BF),
        compiler_params=_params(("parallel", "arbitrary")),
        name="flash_prompt",
    )(q, k, v)


def _mix_out_kernel(x_ref, sh_ref, sc_ref, gt_ref, gmix_ref, wg_ref, bg_ref, oa_ref, ob_ref, oc_ref,
                    wpa_ref, wpb_ref, wpc_ref, wo_ref, o_ref):
    x3 = x_ref[...]
    bs, rb, D = x3.shape
    h = _modulated_norm(x3, gmix_ref, sc_ref, sh_ref)
    m = jnp.zeros((bs * rb, D), F32)
    for b, (o_r, w_r) in enumerate(((oa_ref, wpa_ref), (ob_ref, wpb_ref), (oc_ref, wpc_ref))):
        zg = _dot(h, wg_ref[:, b * D:(b + 1) * D]) + bg_ref[:, b * D:(b + 1) * D]
        m = m + _sigmoid(zg) * _dot(o_r[...], w_r[...])
    out = _dot(m.astype(BF), wo_ref[...])
    o_ref[...] = x3 + gt_ref[...] * out.reshape(bs, rb, D)


def _mix_out_call(x, mod, wts, oa, ob, oc, tm):
    S, R, D = x.shape
    bs, rb = _row_blocking(S, R, tm)
    per_seq = R // rb
    names = ("g_mix", "w_gate", "b_gate")
    wnames = ("w_pa", "w_pb", "w_pc", "w_o")
    return pl.pallas_call(
        _mix_out_kernel,
        grid=(S * R // tm,),
        in_specs=[_x_spec(S, R, D, bs, rb), _mod_spec(D, bs, per_seq, 3), _mod_spec(D, bs, per_seq, 4),
                  _mod_spec(D, bs, per_seq, 5)]
                 + [_const_spec(wts[n].shape) for n in names]
                 + [_rows_spec(oa.shape[1], tm), _rows_spec(ob.shape[1], tm), _rows_spec(oc.shape[1], tm)]
                 + [_const_spec(wts[n].shape) for n in wnames],
        out_specs=_x_spec(S, R, D, bs, rb),
        out_shape=jax.ShapeDtypeStruct((S, R, D), F32),
        compiler_params=_params(("parallel",)),
        name="mix_out",
    )(x, mod, mod, mod, *[wts[n] for n in names], oa, ob, oc, *[wts[n] for n in wnames])


def _final_kernel(x_ref, g_ref, o_ref):
    o_ref[...] = _rms(x_ref[...], g_ref[...])


def _final_call(x2, g, tm):
    N, D = x2.shape
    return pl.pallas_call(
        _final_kernel,
        grid=(N // tm,),
        in_specs=[_rows_spec(D, tm), _const_spec((1, D))],
        out_specs=_rows_spec(D, tm),
        out_shape=jax.ShapeDtypeStruct((N, D), F32),
        compiler_params=_params(("parallel",)),
        name="final_norm",
    )(x2, g)


def _page_copy(cache, l, page, dst, sem):
    return pltpu.make_async_copy(cache.at[l, page], dst, sem)


def _suffix_kernel(pt_cur, pt_next, cache, u12_ref, vs_ref, o_ref, buf, sem, *, n_pages):
    l = pl.program_id(0)
    b = pl.program_id(1)
    nb = pl.num_programs(1)
    nl = pl.num_programs(0)
    step = l * nb + b
    slot = step % 2

    def start(pt_ref, layer, dst_slot):
        for p in range(n_pages):
            _page_copy(cache, layer, pt_ref[0, 0, p], buf.at[dst_slot, p], sem.at[dst_slot]).start()

    @pl.when(step == 0)
    def _():
        start(pt_cur, l, slot)

    @pl.when(step + 1 < nl * nb)
    def _():
        start(pt_next, jnp.where(b + 1 < nb, l, l + 1), 1 - slot)

    for p in range(n_pages):
        _page_copy(cache, l, 0, buf.at[slot, p], sem.at[slot]).wait()

    x = buf[slot].reshape(n_pages * SUBLANE, LANE)
    hi, mid, lo = _split3(x)
    y = _dot(hi, u12_ref[...]) + _dot(mid, u12_ref[...]) + _dot(lo, u12_ref[...])
    within, totals = y[:, 0:LANE], y[:, LANE:2 * LANE]
    o_ref[0, 0] = within + _dot_exact_rhs(vs_ref[...], totals)


def _suffix_call(logf_cache, pt3):
    L, n_pool, page, hf = logf_cache.shape
    B, _, n_pages = pt3.shape
    assert page * hf == SUBLANE * LANE
    pos_per_row = LANE // hf
    cache = logf_cache.reshape(L, n_pool, SUBLANE, LANE)
    lane = np.arange(LANE)
    same_head = (lane[:, None] % hf) == (lane[None, :] % hf)
    u1 = same_head & (lane[:, None] > lane[None, :])
    u12 = jnp.asarray(np.concatenate([u1, same_head], axis=1), BF)
    rows = np.arange(n_pages * SUBLANE)
    vs = jnp.asarray(rows[None, :] > rows[:, None], BF)
    del pos_per_row
    return pl.pallas_call(
        functools.partial(_suffix_kernel, n_pages=n_pages),
        grid=(L, B),
        in_specs=[pl.BlockSpec((1, 1, n_pages), lambda l, b: (b, 0, 0), memory_space=pltpu.SMEM),
                  pl.BlockSpec((1, 1, n_pages), lambda l, b: ((b + 1) % B, 0, 0), memory_space=pltpu.SMEM),
                  pl.BlockSpec(memory_space=pl.ANY),
                  _const_spec(u12.shape), _const_spec(vs.shape)],
        out_specs=pl.BlockSpec((1, 1, n_pages * SUBLANE, LANE), lambda l, b: (l, b, 0, 0)),
        out_shape=jax.ShapeDtypeStruct((L, B, n_pages * SUBLANE, LANE), F32),
        scratch_shapes=[pltpu.VMEM((2, n_pages, SUBLANE, LANE), F32), pltpu.SemaphoreType.DMA((2,))],
        compiler_params=_params(("arbitrary", "arbitrary")),
        name="logf_suffix",
    )(pt3, pt3, cache, u12, vs)


def _decode_kernel(pt_cur, pt_next, qlat_ref, qrope_ref, fq_ref, cumb_ref, rt_ref,
                   cn_ref, krn_ref, fkn_ref, fvn_ref, rtn_ref, wuv_ref,
                   lat_hbm, kr_hbm, fk_hbm, fv_hbm,
                   om_ref, of_ref,
                   lat_buf, kr_buf, fk_buf, fv_buf, sems,
                   st_m, st_f, acc_m, acc_f, *, layer, chunk, n_chunks, heads, tokens):
    b = pl.program_id(0)
    nb = pl.num_programs(0)
    caches = (lat_hbm, kr_hbm, fk_hbm, fv_hbm)
    bufs = (lat_buf, kr_buf, fk_buf, fv_buf)
    keys = chunk * lat_buf.shape[2]
    page = lat_buf.shape[2]

    def start(pt_ref, c, slot):
        for p in range(chunk):
            pg = pt_ref[0, 0, c * chunk + p]
            for a in range(4):
                _page_copy(caches[a], layer, pg, bufs[a].at[slot, p], sems.at[slot, a]).start()

    def wait(slot):
        for p in range(chunk):
            for a in range(4):
                _page_copy(caches[a], layer, 0, bufs[a].at[slot, p], sems.at[slot, a]).wait()

    @pl.when(b == 0)
    def _():
        start(pt_cur, 0, 0)

    qlat = qlat_ref[0]
    qrope = qrope_ref[0]
    fq = fq_ref[0]
    cumb = cumb_ref[0]
    rows = qlat.shape[0]

    st_m[0] = jnp.full((rows, LANE), -jnp.inf, F32)
    st_m[1] = jnp.zeros((rows, LANE), F32)
    st_f[0] = jnp.full((rows, LANE), -jnp.inf, F32)
    st_f[1] = jnp.zeros((rows, LANE), F32)
    acc_m[...] = jnp.zeros_like(acc_m)
    acc_f[...] = jnp.zeros_like(acc_f)

    def expand_heads(rt):
        w = rt.shape[1]
        return jnp.concatenate([jnp.broadcast_to(rt[hh:hh + 1, :], (tokens, w)) for hh in range(heads)], axis=0)

    def update(st, acc, s, v):
        m_old = st[0][:, 0:1]
        l_old = st[1][:, 0:1]
        m_new = jnp.maximum(m_old, jnp.max(s, axis=-1, keepdims=True))
        alpha = jnp.exp(m_old - m_new)
        p = jnp.exp(s - m_new)
        l_new = alpha * l_old + jnp.sum(p, axis=-1, keepdims=True)
        acc[...] = alpha * acc[...] + _dot(p.astype(BF), v)
        st[0] = jnp.broadcast_to(m_new, (rows, LANE))
        st[1] = jnp.broadcast_to(l_new, (rows, LANE))

    def body(c, _):
        slot = c % 2

        @pl.when(c + 1 < n_chunks)
        def _():
            start(pt_cur, c + 1, 1 - slot)

        @pl.when(jnp.logical_and(c + 1 == n_chunks, b + 1 < nb))
        def _():
            start(pt_next, 0, 1 - slot)

        wait(slot)
        lat = lat_buf[slot].reshape(keys, lat_buf.shape[3]).astype(BF)
        kr = kr_buf[slot].reshape(keys, kr_buf.shape[3]).astype(BF)
        fk = fk_buf[slot].reshape(keys, LANE).astype(BF)
        fv = fv_buf[slot].reshape(keys, LANE).astype(BF)
        s_m = _dot_nt(qlat, lat) + _dot_nt(qrope, kr)
        update(st_m, acc_m, s_m, lat)
        rt = rt_ref[0, c]
        s_f = _dot_nt(fq, fk) + expand_heads(rt) + jnp.concatenate([cumb] * chunk, axis=1)
        update(st_f, acc_f, s_f, fv)
        return 0

    lax.fori_loop(0, n_chunks, body, 0)

    r = lax.broadcasted_iota(jnp.int32, (rows, page), 0) % tokens
    col = lax.broadcasted_iota(jnp.int32, (rows, page), 1)
    valid = col <= r
    cn = cn_ref[0]
    s_m = _dot_nt(qlat, cn) + _dot_nt(qrope, krn_ref[0])
    update(st_m, acc_m, jnp.where(valid, s_m, -jnp.inf), cn)
    s_f = _dot_nt(fq, fkn_ref[0]) + expand_heads(rtn_ref[0]) + cumb
    update(st_f, acc_f, jnp.where(valid, s_f, -jnp.inf), fvn_ref[0])

    o_lat = (acc_m[...] / st_m[1][:, 0:1]).astype(BF)
    o_fox = acc_f[...] / st_f[1][:, 0:1]
    half = LANE // 2
    lane = lax.broadcasted_iota(jnp.int32, (tokens, LANE), 1)
    group = heads // 2
    for hh in range(heads):
        rs = slice(hh * tokens, (hh + 1) * tokens)
        om_ref[0, :, hh * LANE:(hh + 1) * LANE] = _dot(o_lat[rs, :], wuv_ref[hh]).astype(BF)
        of = o_fox[rs, :]
        if hh >= group:
            of = pltpu.roll(of, half, axis=1)
        of_ref[0, :, hh * LANE:(hh + 1) * LANE] = jnp.where(lane < half, of, 0.0).astype(BF)


def _decode_call(layer, pt3, dq, rt, caches, wuv_pad, chunk):
    B, _, n_pages = pt3.shape
    lat_c, kr_c, fk_c, fv_c = caches
    page = lat_c.shape[2]
    heads = wuv_pad.shape[0]
    rows = dq["qlat"].shape[1]
    tokens = rows // heads
    assert n_pages % chunk == 0 and (n_pages // chunk) % 2 == 0
    per_seq = lambda shape: pl.BlockSpec((1,) + shape, lambda b: (b,) + (0,) * len(shape))
    names = ("qlat", "qrope", "fq", "cumb")
    new_names = ("cn", "krn", "fkn", "fvn", "rtn")
    in_specs = ([pl.BlockSpec((1, 1, n_pages), lambda b: (b, 0, 0), memory_space=pltpu.SMEM),
                 pl.BlockSpec((1, 1, n_pages), lambda b: ((b + 1) % B, 0, 0), memory_space=pltpu.SMEM)]
                + [per_seq(dq[n].shape[1:]) for n in names]
                + [per_seq(rt.shape[1:])]
                + [per_seq(dq[n].shape[1:]) for n in new_names]
                + [_const_spec(wuv_pad.shape)]
                + [pl.BlockSpec(memory_space=pl.ANY)] * 4)
    ow = heads * LANE
    return pl.pallas_call(
        functools.partial(_decode_kernel, layer=layer, chunk=chunk, n_chunks=n_pages // chunk,
                          heads=heads, tokens=tokens),
        grid=(B,),
        in_specs=in_specs,
        out_specs=[per_seq((tokens, ow)), per_seq((tokens, ow))],
        out_shape=[jax.ShapeDtypeStruct((B, tokens, ow), BF)] * 2,
        scratch_shapes=[pltpu.VMEM((2, chunk, page, lat_c.shape[3]), F32),
                        pltpu.VMEM((2, chunk, page, kr_c.shape[3]), F32),
                        pltpu.VMEM((2, chunk, page, LANE), F32),
                        pltpu.VMEM((2, chunk, page, LANE), F32),
                        pltpu.SemaphoreType.DMA((2, 4)),
                        pltpu.VMEM((2, rows, LANE), F32), pltpu.VMEM((2, rows, LANE), F32),
                        pltpu.VMEM((rows, lat_c.shape[3]), F32), pltpu.VMEM((rows, LANE), F32)],
        compiler_params=_params(("arbitrary",)),
        name="paged_decode",
    )(pt3, pt3, *[dq[n] for n in names], rt, *[dq[n] for n in new_names], wuv_pad,
      lat_c, kr_c, fk_c, fv_c)


def _pad_cols(w, width):
    return jnp.pad(w, ((0, 0), (0, width - w.shape[1])))


def _rot_half_cols(w):
    half = w.shape[1] // 2
    return jnp.concatenate([-w[:, half:], w[:, :half]], axis=1)


def _pack_layer(l, P, dims):
    D, H, nope, rope, lat = dims["D"], dims["H"], dims["nope"], dims["rope"], dims["lat"]
    hf, hk, hd, cw = dims["HF"], dims["HK"], dims["HD"], dims["conv"]
    grp = hf // hk
    w_in = P["w_in"][l]
    splits = np.cumsum([dims["q_rank"], lat, rope, hf * hd, hk * hd, hk * hd, hf, cw, cw, cw])
    zq, zkv, zkr, zfq, zfk, zfv, zff, zb, zc, zh, zg = jnp.split(w_in, splits, axis=1)
    misc = jnp.concatenate([_pad_cols(zkr, LANE), _pad_cols(_rot_half_cols(zkr), LANE), _pad_cols(zff, LANE)], axis=1)
    fq_scale = hd ** -0.5
    fq_pad = jnp.concatenate([_pad_cols(zfq[:, h * hd:(h + 1) * hd] * fq_scale, LANE) for h in range(hf)], axis=1)
    fk_pad = jnp.concatenate([_pad_cols(zfk[:, g * hd:(g + 1) * hd], LANE) for g in range(hk)], axis=1)
    fv_pad = jnp.concatenate([_pad_cols(zfv[:, g * hd:(g + 1) * hd], LANE) for g in range(hk)], axis=1)
    fkv = jnp.concatenate([_pad_cols(zfk, LANE), _pad_cols(zfv, LANE)], axis=1)
    w_big = jnp.concatenate([zq, zkv, misc, fq_pad, fkv, fk_pad, fv_pad, zb, zc, zh], axis=1).astype(BF)

    w_uq = P["w_uq"][l].reshape(dims["q_rank"], H, nope + rope)
    zpad = jnp.zeros((dims["q_rank"], LANE - nope - rope), F32)
    zn = jnp.zeros((dims["q_rank"], nope), F32)
    zr = jnp.zeros((dims["q_rank"], rope), F32)
    wa = jnp.concatenate([jnp.concatenate([w_uq[:, h, nope:], zpad, w_uq[:, h, :nope]], axis=1) for h in range(H)], axis=1)
    wb = jnp.concatenate([jnp.concatenate([_rot_half_cols(w_uq[:, h, nope:]), zpad, zn], axis=1) for h in range(H)], axis=1)
    del zr
    w_uk = P["w_uk"][l]
    w_uv = P["w_uv"][l]
    vdim = w_uv.shape[2]
    zl = jnp.zeros((lat, LANE - nope), F32)
    wk_pad = jnp.concatenate([jnp.concatenate([zl, w_uk[:, h, :]], axis=1) for h in range(H)], axis=1)
    wv_pad = jnp.concatenate([_pad_cols(w_uv[:, h, :], LANE) for h in range(H)], axis=1)
    wukt_pad = jnp.stack([jnp.concatenate([zl, w_uk[:, h, :]], axis=1).T for h in range(H)], axis=0)
    wuv_pad = jnp.stack([_pad_cols(w_uv[:, h, :], LANE) for h in range(H)], axis=0)

    def pad_rows(w, per, n):
        return jnp.concatenate([jnp.pad(w[h * per:(h + 1) * per], ((0, LANE - per), (0, 0))) for h in range(n)], axis=0)

    F = P["w_ff1_out"].shape[1]
    out = dict(
        seg=dims["seg"], heads=H, fox_heads=hf, rope=rope, conv=cw, kv_rank=lat,
        mla_scale=float((nope + rope) ** -0.5),
        g_mix=P["g_mix"][l][None, :], w_in=w_big, g_q=P["g_q"][l][None, :], g_kv=P["g_kv"][l][None, :],
        wuq_a=wa.astype(BF), wuq_b=wb.astype(BF), wk_pad=wk_pad.astype(BF), wv_pad=wv_pad.astype(BF),
        wukt_pad=wukt_pad.astype(BF), wuv_pad=wuv_pad.astype(BF),
        b_forget=_pad_cols(P["b_forget"][l][None, :], LANE), w_conv=P["w_conv"][l],
        w_gate=zg.astype(BF), b_gate=P["b_gate"][l][None, :],
        w_pa=pad_rows(P["w_pa"][l], vdim, H).astype(BF), w_pb=pad_rows(P["w_pb"][l], hd, hf).astype(BF),
        w_pc=P["w_pc"][l].astype(BF), w_o=P["w_o"][l].astype(BF),
        g_ffn1=P["g_ffn1"][l][None, :], g_ffn2=P["g_ffn2"][l][None, :],
        ff1=(P["w_ff1_in"][l][:, :F].astype(BF), P["w_ff1_in"][l][:, F:].astype(BF), P["w_ff1_out"][l].astype(BF)),
        ff2=(P["w_ff2_in"][l][:, :F].astype(BF), P["w_ff2_in"][l][:, F:].astype(BF), P["w_ff2_out"][l].astype(BF)),
    )
    return out


def _rope_tables(pos, rope):
    half = rope // 2
    inv = ROPE_THETA ** (-jnp.arange(half, dtype=F32) / half)
    ang = pos.astype(F32)[:, None] * inv[None, :]
    n = pos.shape[0]
    cos = jnp.concatenate([jnp.cos(ang), jnp.cos(ang), jnp.zeros((n, LANE // 2 - rope), F32),
                           jnp.ones((n, LANE // 2), F32)], axis=1)
    sin = jnp.concatenate([jnp.sin(ang), jnp.sin(ang), jnp.zeros((n, LANE - rope), F32)], axis=1)
    return dict(cos=cos, sin=sin)


def _decay_placement(hf, hk):
    grp = hf // hk
    hd = LANE // 2
    pq = np.zeros((3 * LANE, hf * LANE), np.float32)
    pk = np.zeros((3 * LANE, hk * LANE), np.float32)
    ones_row = hf
    for h in range(hf):
        g, j = divmod(h, grp)
        for part in range(3):
            pq[part * LANE + h, h * LANE + hd + part] = 1.0
            pq[ones_row, h * LANE + hd + 3 + 3 * j + part] = 1.0
            pk[part * LANE + h, g * LANE + hd + 3 + 3 * j + part] = -1.0
    for g in range(hk):
        for part in range(3):
            pk[ones_row, g * LANE + hd + part] = 1.0
    return jnp.asarray(pq, BF), jnp.asarray(pk, BF)


def _dims(P, cache_fox_k):
    L, D, _ = P["w_ada"].shape
    lat, H, nope = P["w_uk"].shape[1:]
    q_rank = P["g_q"].shape[1]
    rope = P["w_uq"].shape[2] // H - nope
    hf = P["b_forget"].shape[1]
    hk, hd = cache_fox_k.shape[3], cache_fox_k.shape[4]
    cw = P["w_conv"].shape[2]
    assert nope + rope <= LANE and nope == LANE // 2 and hd == LANE // 2 and hk == 2
    assert hd + 3 + 3 * (hf // hk) <= LANE and hf < LANE
    return dict(L=L, D=D, H=H, nope=nope, rope=rope, lat=lat, q_rank=q_rank, HF=hf, HK=hk, HD=hd, conv=cw,
                seg=_Seg(q_rank, lat, hf, cw))


def _layer_dense_pre(x, mod, W, tm):
    return _ffn_call(x, mod, 0, W["g_ffn1"], *W["ff1"], tm)


def _layer_dense_post(x, mod, W, oa, ob, oc, tm):
    x = _mix_out_call(x, mod, W, oa, ob, oc, tm)
    return _ffn_call(x, mod, 6, W["g_ffn2"], *W["ff2"], tm)


def kernel(x_prompt, x_sample, cache_mla_latent, cache_mla_krope, cache_fox_k, cache_fox_v, cache_fox_logf,
           state_conv, page_table, c_prompt, c_sample, w_ada, b_ada, g_ffn1, w_ff1_in, w_ff1_out, g_mix, w_in,
           g_q, w_uq, g_kv, w_uk, w_uv, b_forget, w_conv, b_gate, w_pa, w_pb, w_pc, w_o, g_ffn2, w_ff2_in,
           w_ff2_out, g_final):
    P = dict(w_ada=w_ada, b_ada=b_ada, g_ffn1=g_ffn1, w_ff1_in=w_ff1_in, w_ff1_out=w_ff1_out, g_mix=g_mix,
             w_in=w_in, g_q=g_q, w_uq=w_uq, g_kv=g_kv, w_uk=w_uk, w_uv=w_uv, b_forget=b_forget, w_conv=w_conv,
             b_gate=b_gate, w_pa=w_pa, w_pb=w_pb, w_pc=w_pc, w_o=w_o, g_ffn2=g_ffn2, w_ff2_in=w_ff2_in,
             w_ff2_out=w_ff2_out)
    dims = _dims(P, cache_fox_k)
    L, D, H, hf, hk, hd = dims["L"], dims["D"], dims["H"], dims["HF"], dims["HK"], dims["HD"]
    lat, rope, cw = dims["lat"], dims["rope"], dims["conv"]
    Bp, T, _ = x_prompt.shape
    B, Ts, _ = x_sample.shape
    assert Bp == 1 and Ts == SUBLANE
    n_pages, page = page_table.shape[1], cache_mla_latent.shape[2]
    past = n_pages * page
    Ns = B * Ts

    tm_p = min(512, T)
    tm_in = min(256, T)
    tm_s = min(512, Ns)
    tq = min(512, T)
    chunk = min(8, n_pages // 2)

    c_rows = jnp.concatenate([jnp.broadcast_to(c_prompt, (SUBLANE, D)), c_sample], axis=0)
    mod_all = _mod_call(c_rows, w_ada.astype(BF), b_ada)
    layers = [_pack_layer(l, P, dims) for l in range(L)]
    pq, pk = _decay_placement(hf, hk)
    tri = lambda n: jnp.asarray(np.tril(np.ones((n, n), np.float32)), BF)
    seq_tri = jnp.asarray(np.kron(np.eye(tm_s // Ts, dtype=np.float32), np.tril(np.ones((Ts, Ts), np.float32))), BF)

    tab_p = _rope_tables(jnp.arange(T, dtype=jnp.int32), rope)
    x = x_prompt
    p_state = []
    for l in range(L):
        W = layers[l]
        mod = mod_all[l, 0:1][:, None, :]
        x = _layer_dense_pre(x, mod, W, tm_p)
        (c_new, kr_new, fk, fv, logf, oconv, utail, qm, km, vm, qf, kf, vf) = _mix_in_call(
            x, mod, W, tab_p, True, tm_in, dict(ltri=tri(tm_in), pq=pq, pk=pk))
        o_mla = _flash_call(qm, km, vm, tq, tq)
        o_fox = _flash_call(qf, kf, vf, tq, tq)
        x = _layer_dense_post(x, mod, W, o_mla, o_fox, oconv, tm_p)
        p_state.append((c_new.reshape(1, T, lat), kr_new.reshape(1, T, rope),
                        fk.reshape(1, T, hk, hd), fv.reshape(1, T, hk, hd), logf.reshape(1, T, hf),
                        utail[-1, SUBLANE - 2:, :][None]))
    y_prompt = _final_call(x.reshape(T, D), g_final[None, :], tm_p).reshape(1, T, D)

    pt3 = page_table.reshape(B, 1, n_pages)
    suffix = _suffix_call(cache_fox_logf, pt3)
    rt_all = jnp.transpose(suffix.reshape(L, B, n_pages // chunk, chunk * page, hf),
                           (0, 1, 2, 4, 3))
    caches = (cache_mla_latent, cache_mla_krope,
              cache_fox_k.reshape(L, -1, page, hk * hd), cache_fox_v.reshape(L, -1, page, hk * hd))
    tab_s = _rope_tables(past + (jnp.arange(Ns, dtype=jnp.int32) % Ts), rope)
    x = x_sample
    s_state = []
    for l in range(L):
        W = layers[l]
        mod = mod_all[l, SUBLANE:][:, None, :]
        x = _layer_dense_pre(x, mod, W, tm_s)
        st = state_conv[l]
        zero = jnp.zeros((B, Ts - 2, cw), F32)
        halo_a = jnp.concatenate([st, zero], axis=1).reshape(Ns, cw)
        halo_b = jnp.concatenate([st[:, 1:2], jnp.zeros((B, Ts - 1, cw), F32)], axis=1).reshape(Ns, cw)
        (c_new, kr_new, fk, fv, logf, oconv, u_all, qm, qlat, qf, cum) = _mix_in_call(
            x, mod, W, tab_s, False, tm_s, dict(ltri=seq_tri, halo_a=halo_a, halo_b=halo_b))

        def head_major(a):
            w = a.shape[1] // H
            return a.reshape(B, Ts, H, w).transpose(0, 2, 1, 3).reshape(B, H * Ts, w)

        fq_hm = head_major(qf)[:, :, :hd].reshape(B, hf, Ts, hd)
        zq_ = jnp.zeros_like(fq_hm[:, :hf // hk])
        fq_dec = jnp.concatenate([jnp.concatenate([fq_hm[:, :hf // hk], zq_], axis=-1),
                                  jnp.concatenate([zq_, fq_hm[:, hf // hk:]], axis=-1)], axis=1)
        cum_new = cum[:, :hf].reshape(B, Ts, hf)
        cum_hm = cum_new.transpose(0, 2, 1)
        pad_keys = lambda a: jnp.pad(a.reshape(B, Ts, -1).astype(BF), ((0, 0), (0, page - Ts), (0, 0)))
        dq = dict(
            qlat=head_major(qlat), qrope=head_major(qm)[:, :, :rope],
            fq=fq_dec.reshape(B, hf * Ts, LANE),
            cumb=jnp.broadcast_to(cum_hm.reshape(B, hf * Ts, 1), (B, hf * Ts, LANE)),
            cn=pad_keys(c_new), krn=pad_keys(kr_new), fkn=pad_keys(fk), fvn=pad_keys(fv),
            rtn=jnp.pad(-cum_hm, ((0, 0), (0, 0), (0, page - Ts))),
        )
        o_mla, o_fox = _decode_call(l, pt3, dq, rt_all[l], caches, W["wuv_pad"], chunk)
        x = _layer_dense_post(x, mod, W, o_mla.reshape(Ns, H * LANE), o_fox.reshape(Ns, hf * LANE), oconv, tm_s)
        s_state.append((c_new.reshape(B, Ts, lat), kr_new.reshape(B, Ts, rope),
                        fk.reshape(B, Ts, hk, hd), fv.reshape(B, Ts, hk, hd), logf.reshape(B, Ts, hf),
                        u_all.reshape(B, Ts, cw)[:, Ts - 2:, :]))
    y_sample = _final_call(x.reshape(Ns, D), g_final[None, :], tm_s).reshape(B, Ts, D)

    stack = lambda states, i: jnp.stack([s[i] for s in states], axis=0)
    return ((y_prompt, y_sample) + tuple(stack(p_state, i) for i in range(6))
            + tuple(stack(s_state, i) for i in range(6)))
```

```python
import functools

import numpy as np
import jax
import jax.numpy as jnp
from jax import lax
from jax.experimental import pallas as pl
from jax.experimental.pallas import tpu as pltpu

F32 = jnp.float32
BF = jnp.bfloat16

RMS_EPS = 1e-6
ROPE_THETA = 10000.0
LANE = 128
SUBLANE = 8
VMEM_LIMIT = 56 * 1024 * 1024

NT_DIMS = (((1,), (1,)), ((), ()))
LOG2E = 1.4426950408889634
ONES_ROW = LANE // 2


def _dot(a, b):
    return jnp.dot(a, b, preferred_element_type=F32)


def _dot_nt(a, b):
    return lax.dot_general(a, b, NT_DIMS, preferred_element_type=F32)


def _split3(x):
    hi = x.astype(BF)
    r1 = x - hi.astype(F32)
    mid = r1.astype(BF)
    lo = (r1 - mid.astype(F32)).astype(BF)
    return hi, mid, lo


def _dot_exact_rhs(a_bf, x):
    hi, mid, lo = _split3(x)
    return _dot(a_bf, hi) + _dot(a_bf, mid) + _dot(a_bf, lo)


def _sigmoid(x):
    return 1.0 / (1.0 + jnp.exp(-x))


def _rms(x, g):
    ms = jnp.mean(x * x, axis=-1, keepdims=True)
    return x * lax.rsqrt(ms + RMS_EPS) * g


def _const_spec(shape):
    nd = len(shape)
    return pl.BlockSpec(shape, lambda *_: (0,) * nd, pipeline_mode=pl.Buffered(1))


def _params(sem):
    return pltpu.CompilerParams(dimension_semantics=sem, vmem_limit_bytes=VMEM_LIMIT)


def _row_blocking(S, R, tm):
    if R >= tm:
        assert R % tm == 0
        return 1, tm
    assert tm % R == 0 and S % (tm // R) == 0
    return tm // R, R


def _x_spec(S, R, D, bs, rb):
    per_seq = R // rb
    return pl.BlockSpec((bs, rb, D), lambda i: (i // per_seq, i % per_seq, 0))


def _mod_spec(D, bs, per_seq, k):
    return pl.BlockSpec((bs, 1, D), lambda i: (i // per_seq, 0, k))


def _rows_spec(width, tm):
    return pl.BlockSpec((tm, width), lambda i: (i, 0))


def _modulated_norm(x3, g_ref, sc_ref, sh_ref):
    bs, rb, D = x3.shape
    h3 = _rms(x3, g_ref[...].reshape(1, 1, D)) * (1.0 + sc_ref[...]) + sh_ref[...]
    return h3.reshape(bs * rb, D).astype(BF)


def _mod_kernel(c_ref, w_ref, b_ref, o_ref):
    c = c_ref[...]
    a = (c * _sigmoid(c)).astype(BF)
    o_ref[0] = _dot(a, w_ref[0]) + b_ref[0]


def _mod_call(c_rows, w_ada, b_ada):
    L, D, M = w_ada.shape
    Rc = c_rows.shape[0]
    tn = D
    return pl.pallas_call(
        _mod_kernel,
        grid=(L, M // tn),
        in_specs=[pl.BlockSpec((Rc, D), lambda l, j: (0, 0)),
                  pl.BlockSpec((1, D, tn), lambda l, j: (l, 0, j)),
                  pl.BlockSpec((1, 1, tn), lambda l, j: (l, 0, j))],
        out_specs=pl.BlockSpec((1, Rc, tn), lambda l, j: (l, 0, j)),
        out_shape=jax.ShapeDtypeStruct((L, Rc, M), F32),
        compiler_params=_params(("parallel", "parallel")),
        name="ada_mod",
    )(c_rows, w_ada, b_ada.reshape(L, 1, M))


def _ffn_kernel(x_ref, sh_ref, sc_ref, gt_ref, g_ref, wg_ref, wu_ref, wo_ref, o_ref, *, n_chunks):
    x3 = x_ref[...]
    bs, rb, D = x3.shape
    h = _modulated_norm(x3, g_ref, sc_ref, sh_ref)
    fc = wg_ref.shape[1] // n_chunks
    acc = jnp.zeros((bs * rb, D), F32)
    for c in range(n_chunks):
        g = _dot(h, wg_ref[:, c * fc:(c + 1) * fc])
        u = _dot(h, wu_ref[:, c * fc:(c + 1) * fc])
        a = (g * _sigmoid(g) * u).astype(BF)
        acc = acc + _dot(a, wo_ref[c * fc:(c + 1) * fc, :])
    o_ref[...] = x3 + 0.5 * gt_ref[...] * acc.reshape(bs, rb, D)


def _ffn_call(x, mod, k0, g, wg, wu, wo, tm):
    S, R, D = x.shape
    F = wg.shape[1]
    bs, rb = _row_blocking(S, R, tm)
    per_seq = R // rb
    n_chunks = 2 if F % (2 * LANE) == 0 else 1
    return pl.pallas_call(
        functools.partial(_ffn_kernel, n_chunks=n_chunks),
        grid=(S * R // tm,),
        in_specs=[_x_spec(S, R, D, bs, rb),
                  _mod_spec(D, bs, per_seq, k0), _mod_spec(D, bs, per_seq, k0 + 1),
                  _mod_spec(D, bs, per_seq, k0 + 2),
                  _const_spec((1, D)), _const_spec((D, F)), _const_spec((D, F)), _const_spec((F, D))],
        out_specs=_x_spec(S, R, D, bs, rb),
        out_shape=jax.ShapeDtypeStruct((S, R, D), F32),
        compiler_params=_params(("parallel",)),
        name="ffn",
    )(x, mod, mod, mod, g, wg, wu, wo)


class _Seg:
    def __init__(self, q_rank, kv_rank, hq, conv):
        widths = dict(zq=q_rank, zkv=kv_rank, misc=3 * LANE, fq=hq * LANE, fkv=2 * LANE,
                      fkp=2 * LANE, conv=3 * conv)
        self.off = {}
        o = 0
        for name, w in widths.items():
            self.off[name] = (o, o + w)
            o += w
        self.total = o


def _with_ones_rows(vt):
    row = lax.broadcasted_iota(jnp.int32, vt.shape, 0) % LANE
    return jnp.where(row == ONES_ROW, 1.0, vt)


def _mix_in_kernel(*refs, seg, heads, prompt, scale):
    it = iter(refs)
    x_ref, sh_ref, sc_ref, gmix_ref, w_ref = (next(it) for _ in range(5))
    gq_ref, wuqa_ref, wuqb_ref, gkv_ref = (next(it) for _ in range(4))
    if prompt:
        wk_ref, wvt_ref, wfvt_ref = next(it), next(it), next(it)
    else:
        wukt_ref = next(it)
    bf_ref, wconv_ref, cos_ref, sin_ref, ltri_ref = (next(it) for _ in range(5))
    if prompt:
        pq_ref, pk_ref = next(it), next(it)
    else:
        halo_a_ref, halo_b_ref = next(it), next(it)
    c_out, kr_out, fk_out, fv_out, lf_out, oconv_out, utail_out, qm_out = (next(it) for _ in range(8))
    if prompt:
        km_out, vm_out, qf_out, kf_out, vf_out = (next(it) for _ in range(5))
    else:
        qlat_out, qf_out, cum_out = (next(it) for _ in range(3))
    ubuf = next(it)
    if prompt:
        cum_carry = next(it)

    i = pl.program_id(0)
    x3 = x_ref[...]
    h = _modulated_norm(x3, gmix_ref, sc_ref, sh_ref)
    tm = h.shape[0]

    def proj(name):
        a, b = seg.off[name]
        return _dot(h, w_ref[:, a:b])

    cos = cos_ref[...]
    sin = sin_ref[...]

    cq = _rms(proj("zq"), gq_ref[...]).astype(BF)
    qa = _dot(cq, wuqa_ref[...])
    qb = _dot(cq, wuqb_ref[...])
    q_heads = []
    for hh in range(heads):
        sl = slice(hh * LANE, (hh + 1) * LANE)
        qh = ((qa[:, sl] * cos + qb[:, sl] * sin) * scale).astype(BF)
        qm_out[:, sl] = qh
        q_heads.append(qh)

    c_new = _rms(proj("zkv"), gkv_ref[...])
    c_out[...] = c_new
    cb = c_new.astype(BF)
    zm = proj("misc")
    kr = zm[:, 0:LANE] * cos + zm[:, LANE:2 * LANE] * sin
    kr_out[...] = kr[:, 0:kr_out.shape[1]]
    if prompt:
        knope = _dot(cb, wk_ref[...])
        for hh in range(heads):
            sl = slice(hh * LANE, (hh + 1) * LANE)
            km_out[:, sl] = (knope[:, sl] + kr).astype(BF)
        vm_out[0] = _with_ones_rows(_dot_nt(wvt_ref[...], cb)).astype(BF)
    else:
        lat = qlat_out.shape[1] // heads
        for hh in range(heads):
            qlat_out[:, hh * lat:(hh + 1) * lat] = _dot(q_heads[hh], wukt_ref[hh]).astype(BF)

    zff = zm[:, 2 * LANE:3 * LANE] + bf_ref[...]
    logf = jnp.minimum(zff, 0.0) - jnp.log1p(jnp.exp(-jnp.abs(zff)))
    lf_out[...] = logf[:, 0:lf_out.shape[1]]
    zfkv = proj("fkv")
    fk_out[...] = zfkv[:, 0:LANE]
    fv_out[...] = zfkv[:, LANE:2 * LANE]
    cum = _dot_exact_rhs(ltri_ref[...], logf)
    zfq = proj("fq")
    if prompt:
        @pl.when(i == 0)
        def _():
            cum_carry[...] = jnp.zeros_like(cum_carry)
        cum = cum + cum_carry[...]
        cum_carry[...] = cum[tm - 1:tm, :]
        zfq = zfq * LOG2E
        hi, mid, lo = _split3(cum * LOG2E)
        lane = lax.broadcasted_iota(jnp.int32, hi.shape, 1)
        hi = jnp.where(lane == heads, jnp.ones_like(hi), hi)
        sq = jnp.concatenate([hi, mid, lo], axis=1)
        qf_out[...] = (zfq + _dot(sq, pq_ref[...])).astype(BF)
        kf_out[...] = (proj("fkp") + _dot(sq, pk_ref[...])).astype(BF)
        vf_out[0] = _with_ones_rows(_dot_nt(wfvt_ref[...], h)).astype(BF)
    else:
        qf_out[...] = zfq.astype(BF)
        cum_out[...] = cum

    zc = proj("conv")
    cw = zc.shape[1] // 3
    zb, u = zc[:, 0:cw], zc[:, cw:2 * cw] * zc[:, 2 * cw:3 * cw]
    @pl.when(i == 0)
    def _():
        ubuf[0:SUBLANE, :] = jnp.zeros((SUBLANE, cw), F32)
    ubuf[SUBLANE:SUBLANE + tm, :] = u
    um1 = ubuf[SUBLANE - 1:SUBLANE - 1 + tm, :]
    um2 = ubuf[SUBLANE - 2:SUBLANE - 2 + tm, :]
    if not prompt:
        r = lax.broadcasted_iota(jnp.int32, u.shape, 0) % SUBLANE
        um2 = jnp.where(r < 2, halo_a_ref[...], um2)
        um1 = jnp.where(r == 0, halo_b_ref[...], um1)
    wc = wconv_ref[...]
    y = wc[0:1, :] * um2 + wc[1:2, :] * um1 + wc[2:3, :] * u
    oconv_out[...] = (zb * y).astype(BF)
    if prompt:
        utail_out[0] = u[tm - SUBLANE:tm, :]
        ubuf[0:SUBLANE, :] = u[tm - SUBLANE:tm, :]
    else:
        utail_out[...] = u


def _mix_in_call(x, mod, wts, tables, prompt, tm, extra):
    S, R, D = x.shape
    N = S * R
    bs, rb = _row_blocking(S, R, tm)
    per_seq = R // rb
    seg, heads = wts["seg"], wts["heads"]
    hq = wts["fox_heads"]
    nblk = N // tm
    kr_w, lf_w, cw = wts["rope"], wts["fox_heads"], wts["conv"]
    in_arrays = [x, mod, mod, wts["g_mix"], wts["w_in"], wts["g_q"], wts["wuq_a"], wts["wuq_b"], wts["g_kv"]]
    in_specs = [_x_spec(S, R, D, bs, rb), _mod_spec(D, bs, per_seq, 3), _mod_spec(D, bs, per_seq, 4),
                _const_spec((1, D)), _const_spec(wts["w_in"].shape), _const_spec(wts["g_q"].shape),
                _const_spec(wts["wuq_a"].shape), _const_spec(wts["wuq_b"].shape), _const_spec(wts["g_kv"].shape)]
    if prompt:
        in_arrays += [wts["wk_pad"], wts["wvt_pad"], wts["wfvt_pad"]]
        in_specs += [_const_spec(wts["wk_pad"].shape), _const_spec(wts["wvt_pad"].shape),
                     _const_spec(wts["wfvt_pad"].shape)]
    else:
        in_arrays += [wts["wukt_pad"]]
        in_specs += [_const_spec(wts["wukt_pad"].shape)]
    in_arrays += [wts["b_forget"], wts["w_conv"], tables["cos"], tables["sin"], extra["ltri"]]
    in_specs += [_const_spec(wts["b_forget"].shape), _const_spec(wts["w_conv"].shape),
                 _rows_spec(LANE, tm), _rows_spec(LANE, tm), _const_spec((tm, tm))]
    if prompt:
        in_arrays += [extra["pq"], extra["pk"]]
        in_specs += [_const_spec(extra["pq"].shape), _const_spec(extra["pk"].shape)]
    else:
        in_arrays += [extra["halo_a"], extra["halo_b"]]
        in_specs += [_rows_spec(cw, tm), _rows_spec(cw, tm)]

    lat = wts["kv_rank"]
    out = [((N, lat), F32), ((N, kr_w), F32), ((N, LANE), F32), ((N, LANE), F32), ((N, lf_w), F32),
           ((N, cw), BF)]
    out_specs = [_rows_spec(lat, tm), _rows_spec(kr_w, tm), _rows_spec(LANE, tm), _rows_spec(LANE, tm),
                 _rows_spec(lf_w, tm), _rows_spec(cw, tm)]
    if prompt:
        out += [((nblk, SUBLANE, cw), F32)]
        out_specs += [pl.BlockSpec((1, SUBLANE, cw), lambda i: (i, 0, 0))]
    else:
        out += [((N, cw), F32)]
        out_specs += [_rows_spec(cw, tm)]
    out += [((N, heads * LANE), BF)]
    out_specs += [_rows_spec(heads * LANE, tm)]
    if prompt:
        tile_t = lambda rows: pl.BlockSpec((1, rows, tm), lambda i: (i, 0, 0))
        out += [((N, heads * LANE), BF), ((nblk, heads * LANE, tm), BF), ((N, hq * LANE), BF),
                ((N, 2 * LANE), BF), ((nblk, 2 * LANE, tm), BF)]
        out_specs += [_rows_spec(heads * LANE, tm), tile_t(heads * LANE), _rows_spec(hq * LANE, tm),
                      _rows_spec(2 * LANE, tm), tile_t(2 * LANE)]
    else:
        out += [((N, heads * lat), BF), ((N, hq * LANE), BF), ((N, LANE), F32)]
        out_specs += [_rows_spec(heads * lat, tm), _rows_spec(hq * LANE, tm), _rows_spec(LANE, tm)]
    scratch = [pltpu.VMEM((tm + SUBLANE, cw), F32)]
    if prompt:
        scratch += [pltpu.VMEM((1, LANE), F32)]
    return pl.pallas_call(
        functools.partial(_mix_in_kernel, seg=seg, heads=heads, prompt=prompt,
                          scale=wts["mla_scale"] * (LOG2E if prompt else 1.0)),
        grid=(nblk,),
        in_specs=in_specs,
        out_specs=out_specs,
        out_shape=[jax.ShapeDtypeStruct(s, d) for s, d in out],
        scratch_shapes=scratch,
        compiler_params=_params(("arbitrary",)),
        name="mix_in_prompt" if prompt else "mix_in_sample",
    )(*in_arrays)


FLASH_HEADS = 4


def _flash_kernel(q_ref, k_ref, vt_ref, o_ref, s_scr, p_scr, m_scr, acc_scr, *, shared_kv):
    i = pl.program_id(1)
    tq = q_ref.shape[0]
    tk = vt_ref.shape[2]
    nh = q_ref.shape[1] // LANE
    heads = range(nh)
    qs = [q_ref[:, a * LANE:(a + 1) * LANE] for a in heads]
    kv = [slice(0, LANE) if shared_kv else slice(a * LANE, (a + 1) * LANE) for a in heads]

    def scores(j, slot):
        start = pl.multiple_of(j * tk, tk)
        for a in heads:
            s_scr[slot, a] = _dot_nt(k_ref[pl.ds(start, tk), kv[a]], qs[a])

    def step(j, slot, masked, last):
        rows_per_pass = 64

        def masked_scores(a, rows, cols):
            sa = s_scr[slot, a, rows, cols]
            if masked:
                key = j * tk + rows.start + lax.broadcasted_iota(jnp.int32, sa.shape, 0)
                qry = i * tq + cols.start + lax.broadcasted_iota(jnp.int32, sa.shape, 1)
                sa = jnp.where(key <= qry, sa, -jnp.inf)
            return sa

        alpha, m_new = [], []
        for a in heads:
            m_old = m_scr[a]
            strips = [jnp.max(masked_scores(a, slice(0, tk), slice(c * LANE, (c + 1) * LANE)), axis=0, keepdims=True)
                      for c in range(tq // LANE)]
            mn = jnp.maximum(m_old, jnp.concatenate(strips, axis=1))
            m_scr[a] = mn
            m_new.append(mn)
            alpha.append(jnp.exp2(m_old - mn))
        for a in heads:
            for r in range(tk // rows_per_pass):
                rows = slice(r * rows_per_pass, (r + 1) * rows_per_pass)
                p_scr[a, rows, :] = jnp.exp2(masked_scores(a, rows, slice(0, tq)) - m_new[a]).astype(BF)
        if not last:
            scores(j + 1, 1 - slot)
        for a in heads:
            acc_scr[a] = alpha[a] * acc_scr[a] + _dot(vt_ref[j, kv[a], :], p_scr[a])

    n_diag = tq // tk
    assert n_diag == 2
    m_scr[...] = jnp.full(m_scr.shape, -jnp.inf, F32)
    acc_scr[...] = jnp.zeros(acc_scr.shape, F32)
    scores(0, 0)

    def body(jj, _):
        step(2 * jj, 0, False, False)
        step(2 * jj + 1, 1, False, False)
        return 0

    lax.fori_loop(0, i, body, 0)
    for d in range(n_diag):
        step(2 * i + d, d, True, d == n_diag - 1)
    for a in heads:
        acc = acc_scr[a]
        o_ref[:, a * LANE:(a + 1) * LANE] = (acc / acc[ONES_ROW:ONES_ROW + 1, :]).T.astype(o_ref.dtype)


def _flash_call(q, k, vt, tq):
    N, qw = q.shape
    nk, _, tk = vt.shape
    hq, hk = qw // LANE, k.shape[1] // LANE
    nh = min(FLASH_HEADS, hq)
    shared_kv = hk != hq
    assert tq % tk == 0 and nk * tk == N and hq % nh == 0
    assert (not shared_kv) or hq // hk == nh
    kvw = LANE if shared_kv else nh * LANE
    return pl.pallas_call(
        functools.partial(_flash_kernel, shared_kv=shared_kv),
        grid=(hq // nh, N // tq),
        in_specs=[pl.BlockSpec((tq, nh * LANE), lambda g, i: (i, g)),
                  pl.BlockSpec((N, kvw), lambda g, i: (0, g), pipeline_mode=pl.Buffered(1)),
                  pl.BlockSpec((nk, kvw, tk), lambda g, i: (0, g, 0), pipeline_mode=pl.Buffered(1))],
        out_specs=pl.BlockSpec((tq, nh * LANE), lambda g, i: (i, g)),
        out_shape=jax.ShapeDtypeStruct((N, qw), BF),
        scratch_shapes=[pltpu.VMEM((2, nh, tk, tq), F32), pltpu.VMEM((nh, tk, tq), BF),
                        pltpu.VMEM((nh, 1, tq), F32), pltpu.VMEM((nh, LANE, tq), F32)],
        compiler_params=_params(("parallel", "arbitrary")),
        name="flash_prompt",
    )(q, k, vt)


def _mix_out_kernel(x_ref, sh_ref, sc_ref, gt_ref, gmix_ref, wg_ref, bg_ref, oa_ref, ob_ref, oc_ref,
                    wpa_ref, wpb_ref, wpc_ref, wo_ref, o_ref):
    x3 = x_ref[...]
    bs, rb, D = x3.shape
    h = _modulated_norm(x3, gmix_ref, sc_ref, sh_ref)
    m = jnp.zeros((bs * rb, D), F32)
    for b, (o_r, w_r) in enumerate(((oa_ref, wpa_ref), (ob_ref, wpb_ref), (oc_ref, wpc_ref))):
        zg = _dot(h, wg_ref[:, b * D:(b + 1) * D]) + bg_ref[:, b * D:(b + 1) * D]
        m = m + _sigmoid(zg) * _dot(o_r[...], w_r[...])
    out = _dot(m.astype(BF), wo_ref[...])
    o_ref[...] = x3 + gt_ref[...] * out.reshape(bs, rb, D)


def _mix_out_call(x, mod, wts, oa, ob, oc, tm):
    S, R, D = x.shape
    bs, rb = _row_blocking(S, R, tm)
    per_seq = R // rb
    names = ("g_mix", "w_gate", "b_gate")
    wnames = ("w_pa", "w_pb", "w_pc", "w_o")
    return pl.pallas_call(
        _mix_out_kernel,
        grid=(S * R // tm,),
        in_specs=[_x_spec(S, R, D, bs, rb), _mod_spec(D, bs, per_seq, 3), _mod_spec(D, bs, per_seq, 4),
                  _mod_spec(D, bs, per_seq, 5)]
                 + [_const_spec(wts[n].shape) for n in names]
                 + [_rows_spec(oa.shape[1], tm), _rows_spec(ob.shape[1], tm), _rows_spec(oc.shape[1], tm)]
                 + [_const_spec(wts[n].shape) for n in wnames],
        out_specs=_x_spec(S, R, D, bs, rb),
        out_shape=jax.ShapeDtypeStruct((S, R, D), F32),
        compiler_params=_params(("parallel",)),
        name="mix_out",
    )(x, mod, mod, mod, *[wts[n] for n in names], oa, ob, oc, *[wts[n] for n in wnames])


def _final_kernel(x_ref, g_ref, o_ref):
    o_ref[...] = _rms(x_ref[...], g_ref[...])


def _final_call(x2, g, tm):
    N, D = x2.shape
    return pl.pallas_call(
        _final_kernel,
        grid=(N // tm,),
        in_specs=[_rows_spec(D, tm), _const_spec((1, D))],
        out_specs=_rows_spec(D, tm),
        out_shape=jax.ShapeDtypeStruct((N, D), F32),
        compiler_params=_params(("parallel",)),
        name="final_norm",
    )(x2, g)


def _decode_kernel(pt_cur, pt_next, qlat_ref, qrope_ref, fq_ref, cumb_ref,
                   cn_ref, krn_ref, fkn_ref, fvn_ref, rtn_ref, wuv_ref, usum_ref,
                   lat_hbm, krt_hbm, fkt_hbm, fvt_hbm, lft_hbm,
                   om_ref, of_ref,
                   lat_buf, krt_buf, fkt_buf, fvt_buf, lf_buf, sems, lf_sems,
                   within_scr, total_scr, st_m, st_f, acc_m, acc_f, *, layer, chunk, n_chunks, heads, tokens):
    b = pl.program_id(0)
    nb = pl.num_programs(0)
    caches = (lat_hbm, krt_hbm, fkt_hbm, fvt_hbm)
    bufs = (lat_buf, krt_buf, fkt_buf, fvt_buf)
    n_pages = chunk * n_chunks
    page = lat_buf.shape[2]
    keys = chunk * page
    last = n_chunks - 1
    lf_slot = b % 2

    def page_copy(a, pg, slot, p):
        return pltpu.make_async_copy(caches[a].at[layer, pg], bufs[a].at[slot, p], sems.at[slot, a])

    def lf_copy(pg, slot, p):
        return pltpu.make_async_copy(lft_hbm.at[layer, pg], lf_buf.at[slot, p], lf_sems.at[slot])

    def start_chunk(page_of, slot):
        for p in range(chunk):
            pg = page_of(p)
            for a in range(len(caches)):
                page_copy(a, pg, slot, p).start()

    def wait_chunk(slot):
        for p in range(chunk):
            for a in range(len(caches)):
                page_copy(a, 0, slot, p).wait()

    def start_lf(pt_ref, slot):
        for p in range(n_pages):
            lf_copy(pt_ref[0, 0, p], slot, p).start()

    def wait_lf(slot):
        for p in range(n_pages):
            lf_copy(0, slot, p).wait()

    @pl.when(b == 0)
    def _():
        start_chunk(lambda p: pt_cur[0, 0, last * chunk + p], 0)
        start_lf(pt_cur, 0)

    start_lf(pt_next, 1 - lf_slot)
    wait_lf(lf_slot)
    hi, mid, lo = _split3(lf_buf[lf_slot].reshape(n_pages * heads, LANE))
    usum = usum_ref[...]
    y = _dot(hi, usum) + _dot(mid, usum) + _dot(lo, usum)
    within_scr[...] = y[:, 0:LANE]
    total_scr[...] = y[:, LANE:2 * LANE]

    qlat = qlat_ref[0]
    qrope = qrope_ref[0]
    fq = fq_ref[0]
    cumb = cumb_ref[0]
    rows = qlat.shape[0]

    st_m[0] = jnp.full((rows, LANE), -jnp.inf, F32)
    st_m[1] = jnp.zeros((rows, LANE), F32)
    st_f[0] = jnp.full((rows, LANE), -jnp.inf, F32)
    st_f[1] = jnp.zeros((rows, LANE), F32)
    acc_m[...] = jnp.zeros_like(acc_m)
    acc_f[...] = jnp.zeros_like(acc_f)

    def expand_heads(rt):
        w = rt.shape[1]
        return jnp.concatenate([jnp.broadcast_to(rt[hh:hh + 1, :], (tokens, w)) for hh in range(heads)], axis=0)

    def soften(st, s):
        m_old = st[0][:, 0:1]
        l_old = st[1][:, 0:1]
        m_new = jnp.maximum(m_old, jnp.max(s, axis=-1, keepdims=True))
        alpha = jnp.exp(m_old - m_new)
        p = jnp.exp(s - m_new)
        l_new = alpha * l_old + jnp.sum(p, axis=-1, keepdims=True)
        st[0] = jnp.broadcast_to(m_new, (rows, LANE))
        st[1] = jnp.broadcast_to(l_new, (rows, LANE))
        return p.astype(BF), alpha

    def body(k, carry):
        slot = k % 2
        c = last - k
        wrap = k == last
        c_next = jnp.where(wrap, last, c - 1)

        def next_page(p):
            idx = c_next * chunk + p
            return jnp.where(wrap, pt_next[0, 0, idx], pt_cur[0, 0, idx])

        start_chunk(next_page, 1 - slot)
        wait_chunk(slot)
        lat = lat_buf[slot].reshape(keys, lat_buf.shape[3]).astype(BF)
        krt = jnp.concatenate([krt_buf[slot, p] for p in range(chunk)], axis=1).astype(BF)
        fkt = jnp.concatenate([fkt_buf[slot, p].reshape(LANE, page) for p in range(chunk)], axis=1).astype(BF)
        fvt = jnp.concatenate([fvt_buf[slot, p].reshape(LANE, page) for p in range(chunk)], axis=1).astype(BF)
        rts = [None] * chunk
        for p in reversed(range(chunk)):
            row0 = pl.multiple_of((c * chunk + p) * heads, heads)
            rts[p] = within_scr[pl.ds(row0, heads), :] + carry
            carry = carry + total_scr[pl.ds(row0, heads), :]
        s_m = _dot_nt(qlat, lat) + _dot(qrope, krt)
        s_f = (_dot(fq, fkt) + expand_heads(jnp.concatenate(rts, axis=1))
               + jnp.concatenate([cumb] * chunk, axis=1))
        p_m, alpha_m = soften(st_m, s_m)
        p_f, alpha_f = soften(st_f, s_f)
        acc_m[...] = alpha_m * acc_m[...] + _dot(p_m, lat)
        acc_f[...] = alpha_f * acc_f[...] + _dot_nt(p_f, fvt)
        return carry

    lax.fori_loop(0, n_chunks, body, jnp.zeros((heads, LANE), F32))

    @pl.when(b == nb - 1)
    def _():
        wait_chunk((last + 1) % 2)
        wait_lf(1 - lf_slot)

    r = lax.broadcasted_iota(jnp.int32, (rows, page), 0) % tokens
    col = lax.broadcasted_iota(jnp.int32, (rows, page), 1)
    valid = col <= r
    cn = cn_ref[0]
    fvn = fvn_ref[0]
    s_m = _dot_nt(qlat, cn) + _dot_nt(qrope, krn_ref[0])
    s_f = _dot_nt(fq, fkn_ref[0]) + expand_heads(rtn_ref[0]) + cumb
    p_m, alpha_m = soften(st_m, jnp.where(valid, s_m, -jnp.inf))
    p_f, alpha_f = soften(st_f, jnp.where(valid, s_f, -jnp.inf))
    acc_m[...] = alpha_m * acc_m[...] + _dot(p_m, cn)
    acc_f[...] = alpha_f * acc_f[...] + _dot(p_f, fvn)

    o_lat = (acc_m[...] / st_m[1][:, 0:1]).astype(BF)
    o_fox = acc_f[...] / st_f[1][:, 0:1]
    half = LANE // 2
    lane = lax.broadcasted_iota(jnp.int32, (tokens, LANE), 1)
    group = heads // 2
    for hh in range(heads):
        rs = slice(hh * tokens, (hh + 1) * tokens)
        om_ref[0, :, hh * LANE:(hh + 1) * LANE] = _dot(o_lat[rs, :], wuv_ref[hh]).astype(BF)
        of = o_fox[rs, :]
        if hh >= group:
            of = pltpu.roll(of, half, axis=1)
        of_ref[0, :, hh * LANE:(hh + 1) * LANE] = jnp.where(lane < half, of, 0.0).astype(BF)


def _decode_call(layer, pt3, dq, caches, wuv_pad, chunk):
    B, _, n_pages = pt3.shape
    lat_c, krt_c, fkt_c, fvt_c, lft_c = caches
    page = lat_c.shape[2]
    heads = wuv_pad.shape[0]
    rows = dq["qlat"].shape[1]
    tokens = rows // heads
    assert n_pages % chunk == 0 and (n_pages // chunk) % 2 == 0
    assert lft_c.shape[2] == heads == SUBLANE and page == LANE and fkt_c.shape[2] * fkt_c.shape[3] == LANE
    later = np.arange(page)[:, None] > np.arange(page)[None, :]
    usum = jnp.asarray(np.concatenate([later, np.ones((page, page), bool)], axis=1), BF)
    per_seq = lambda shape: pl.BlockSpec((1,) + shape, lambda b: (b,) + (0,) * len(shape))
    names = ("qlat", "qrope", "fq", "cumb", "cn", "krn", "fkn", "fvn", "rtn")
    in_specs = ([pl.BlockSpec((1, 1, n_pages), lambda b: (b, 0, 0), memory_space=pltpu.SMEM),
                 pl.BlockSpec((1, 1, n_pages), lambda b: ((b + 1) % B, 0, 0), memory_space=pltpu.SMEM)]
                + [per_seq(dq[n].shape[1:]) for n in names]
                + [_const_spec(wuv_pad.shape), _const_spec(usum.shape)]
                + [pl.BlockSpec(memory_space=pl.ANY)] * len(caches))
    ow = heads * LANE
    return pl.pallas_call(
        functools.partial(_decode_kernel, layer=layer, chunk=chunk, n_chunks=n_pages // chunk,
                          heads=heads, tokens=tokens),
        grid=(B,),
        in_specs=in_specs,
        out_specs=[per_seq((tokens, ow)), per_seq((tokens, ow))],
        out_shape=[jax.ShapeDtypeStruct((B, tokens, ow), BF)] * 2,
        scratch_shapes=[pltpu.VMEM((2, chunk) + lat_c.shape[2:], F32),
                        pltpu.VMEM((2, chunk) + krt_c.shape[2:], F32),
                        pltpu.VMEM((2, chunk) + fkt_c.shape[2:], F32),
                        pltpu.VMEM((2, chunk) + fvt_c.shape[2:], F32),
                        pltpu.VMEM((2, n_pages) + lft_c.shape[2:], F32),
                        pltpu.SemaphoreType.DMA((2, 4)), pltpu.SemaphoreType.DMA((2,)),
                        pltpu.VMEM((n_pages * heads, LANE), F32), pltpu.VMEM((n_pages * heads, LANE), F32),
                        pltpu.VMEM((2, rows, LANE), F32), pltpu.VMEM((2, rows, LANE), F32),
                        pltpu.VMEM((rows, lat_c.shape[3]), F32), pltpu.VMEM((rows, LANE), F32)],
        compiler_params=_params(("arbitrary",)),
        name="paged_decode",
    )(pt3, pt3, *[dq[n] for n in names], wuv_pad, usum, *caches)


def _pad_cols(w, width):
    return jnp.pad(w, ((0, 0), (0, width - w.shape[1])))


def _rot_half_cols(w):
    half = w.shape[1] // 2
    return jnp.concatenate([-w[:, half:], w[:, :half]], axis=1)


def _pack_layer(l, P, dims):
    D, H, nope, rope, lat = dims["D"], dims["H"], dims["nope"], dims["rope"], dims["lat"]
    hf, hk, hd, cw = dims["HF"], dims["HK"], dims["HD"], dims["conv"]
    grp = hf // hk
    w_in = P["w_in"][l]
    splits = np.cumsum([dims["q_rank"], lat, rope, hf * hd, hk * hd, hk * hd, hf, cw, cw, cw])
    zq, zkv, zkr, zfq, zfk, zfv, zff, zb, zc, zh, zg = jnp.split(w_in, splits, axis=1)
    misc = jnp.concatenate([_pad_cols(zkr, LANE), _pad_cols(_rot_half_cols(zkr), LANE), _pad_cols(zff, LANE)], axis=1)
    fq_scale = hd ** -0.5
    fq_pad = jnp.concatenate([_pad_cols(zfq[:, h * hd:(h + 1) * hd] * fq_scale, LANE) for h in range(hf)], axis=1)
    fk_pad = jnp.concatenate([_pad_cols(zfk[:, g * hd:(g + 1) * hd], LANE) for g in range(hk)], axis=1)
    fv_pad = jnp.concatenate([_pad_cols(zfv[:, g * hd:(g + 1) * hd], LANE) for g in range(hk)], axis=1)
    fkv = jnp.concatenate([_pad_cols(zfk, LANE), _pad_cols(zfv, LANE)], axis=1)
    w_big = jnp.concatenate([zq, zkv, misc, fq_pad, fkv, fk_pad, zb, zc, zh], axis=1).astype(BF)

    w_uq = P["w_uq"][l].reshape(dims["q_rank"], H, nope + rope)
    zpad = jnp.zeros((dims["q_rank"], LANE - nope - rope), F32)
    zn = jnp.zeros((dims["q_rank"], nope), F32)
    zr = jnp.zeros((dims["q_rank"], rope), F32)
    wa = jnp.concatenate([jnp.concatenate([w_uq[:, h, nope:], zpad, w_uq[:, h, :nope]], axis=1) for h in range(H)], axis=1)
    wb = jnp.concatenate([jnp.concatenate([_rot_half_cols(w_uq[:, h, nope:]), zpad, zn], axis=1) for h in range(H)], axis=1)
    del zr
    w_uk = P["w_uk"][l]
    w_uv = P["w_uv"][l]
    vdim = w_uv.shape[2]
    zl = jnp.zeros((lat, LANE - nope), F32)
    wk_pad = jnp.concatenate([jnp.concatenate([zl, w_uk[:, h, :]], axis=1) for h in range(H)], axis=1)
    wvt_pad = jnp.concatenate([_pad_cols(w_uv[:, h, :], LANE) for h in range(H)], axis=1).T
    wukt_pad = jnp.stack([jnp.concatenate([zl, w_uk[:, h, :]], axis=1).T for h in range(H)], axis=0)
    wuv_pad = jnp.stack([_pad_cols(w_uv[:, h, :], LANE) for h in range(H)], axis=0)

    def pad_rows(w, per, n):
        return jnp.concatenate([jnp.pad(w[h * per:(h + 1) * per], ((0, LANE - per), (0, 0))) for h in range(n)], axis=0)

    F = P["w_ff1_out"].shape[1]
    out = dict(
        seg=dims["seg"], heads=H, fox_heads=hf, rope=rope, conv=cw, kv_rank=lat,
        mla_scale=float((nope + rope) ** -0.5),
        g_mix=P["g_mix"][l][None, :], w_in=w_big, g_q=P["g_q"][l][None, :], g_kv=P["g_kv"][l][None, :],
        wuq_a=wa.astype(BF), wuq_b=wb.astype(BF), wk_pad=wk_pad.astype(BF), wvt_pad=wvt_pad.astype(BF),
        wfvt_pad=fv_pad.T.astype(BF),
        wukt_pad=wukt_pad.astype(BF), wuv_pad=wuv_pad.astype(BF),
        b_forget=_pad_cols(P["b_forget"][l][None, :], LANE), w_conv=P["w_conv"][l],
        w_gate=zg.astype(BF), b_gate=P["b_gate"][l][None, :],
        w_pa=pad_rows(P["w_pa"][l], vdim, H).astype(BF), w_pb=pad_rows(P["w_pb"][l], hd, hf).astype(BF),
        w_pc=P["w_pc"][l].astype(BF), w_o=P["w_o"][l].astype(BF),
        g_ffn1=P["g_ffn1"][l][None, :], g_ffn2=P["g_ffn2"][l][None, :],
        ff1=(P["w_ff1_in"][l][:, :F].astype(BF), P["w_ff1_in"][l][:, F:].astype(BF), P["w_ff1_out"][l].astype(BF)),
        ff2=(P["w_ff2_in"][l][:, :F].astype(BF), P["w_ff2_in"][l][:, F:].astype(BF), P["w_ff2_out"][l].astype(BF)),
    )
    return out


def _rope_tables(pos, rope):
    half = rope // 2
    inv = ROPE_THETA ** (-jnp.arange(half, dtype=F32) / half)
    ang = pos.astype(F32)[:, None] * inv[None, :]
    n = pos.shape[0]
    cos = jnp.concatenate([jnp.cos(ang), jnp.cos(ang), jnp.zeros((n, LANE // 2 - rope), F32),
                           jnp.ones((n, LANE // 2), F32)], axis=1)
    sin = jnp.concatenate([jnp.sin(ang), jnp.sin(ang), jnp.zeros((n, LANE - rope), F32)], axis=1)
    return dict(cos=cos, sin=sin)


def _decay_placement(hf, hk):
    grp = hf // hk
    hd = LANE // 2
    pq = np.zeros((3 * LANE, hf * LANE), np.float32)
    pk = np.zeros((3 * LANE, hk * LANE), np.float32)
    ones_row = hf
    for h in range(hf):
        g, j = divmod(h, grp)
        for part in range(3):
            pq[part * LANE + h, h * LANE + hd + part] = 1.0
            pq[ones_row, h * LANE + hd + 3 + 3 * j + part] = 1.0
            pk[part * LANE + h, g * LANE + hd + 3 + 3 * j + part] = -1.0
    for g in range(hk):
        for part in range(3):
            pk[ones_row, g * LANE + hd + part] = 1.0
    return jnp.asarray(pq, BF), jnp.asarray(pk, BF)


def _dims(P, cache_fox_k):
    L, D, _ = P["w_ada"].shape
    lat, H, nope = P["w_uk"].shape[1:]
    q_rank = P["g_q"].shape[1]
    rope = P["w_uq"].shape[2] // H - nope
    hf = P["b_forget"].shape[1]
    hk, hd = cache_fox_k.shape[3], cache_fox_k.shape[4]
    cw = P["w_conv"].shape[2]
    assert nope + rope <= LANE and nope == LANE // 2 and hd == LANE // 2 and hk == 2
    assert hd + 3 + 3 * (hf // hk) <= LANE and hf < LANE
    return dict(L=L, D=D, H=H, nope=nope, rope=rope, lat=lat, q_rank=q_rank, HF=hf, HK=hk, HD=hd, conv=cw,
                seg=_Seg(q_rank, lat, hf, cw))


def _layer_dense_pre(x, mod, W, tm):
    return _ffn_call(x, mod, 0, W["g_ffn1"], *W["ff1"], tm)


def _layer_dense_post(x, mod, W, oa, ob, oc, tm):
    x = _mix_out_call(x, mod, W, oa, ob, oc, tm)
    return _ffn_call(x, mod, 6, W["g_ffn2"], *W["ff2"], tm)


def kernel(x_prompt, x_sample, cache_mla_latent, cache_mla_krope, cache_fox_k, cache_fox_v, cache_fox_logf,
           state_conv, page_table, c_prompt, c_sample, w_ada, b_ada, g_ffn1, w_ff1_in, w_ff1_out, g_mix, w_in,
           g_q, w_uq, g_kv, w_uk, w_uv, b_forget, w_conv, b_gate, w_pa, w_pb, w_pc, w_o, g_ffn2, w_ff2_in,
           w_ff2_out, g_final):
    P = dict(w_ada=w_ada, b_ada=b_ada, g_ffn1=g_ffn1, w_ff1_in=w_ff1_in, w_ff1_out=w_ff1_out, g_mix=g_mix,
             w_in=w_in, g_q=g_q, w_uq=w_uq, g_kv=g_kv, w_uk=w_uk, w_uv=w_uv, b_forget=b_forget, w_conv=w_conv,
             b_gate=b_gate, w_pa=w_pa, w_pb=w_pb, w_pc=w_pc, w_o=w_o, g_ffn2=g_ffn2, w_ff2_in=w_ff2_in,
             w_ff2_out=w_ff2_out)
    dims = _dims(P, cache_fox_k)
    L, D, H, hf, hk, hd = dims["L"], dims["D"], dims["H"], dims["HF"], dims["HK"], dims["HD"]
    lat, rope, cw = dims["lat"], dims["rope"], dims["conv"]
    Bp, T, _ = x_prompt.shape
    B, Ts, _ = x_sample.shape
    assert Bp == 1 and Ts == SUBLANE
    n_pages, page = page_table.shape[1], cache_mla_latent.shape[2]
    past = n_pages * page
    Ns = B * Ts

    tm_p = min(512, T)
    tm_in = min(256, T)
    tm_s = min(512, Ns)
    tq = min(512, T)
    chunk = min(16, n_pages // 2)

    c_rows = jnp.concatenate([jnp.broadcast_to(c_prompt, (SUBLANE, D)), c_sample], axis=0)
    mod_all = _mod_call(c_rows, w_ada.astype(BF), b_ada)
    layers = [_pack_layer(l, P, dims) for l in range(L)]
    pq, pk = _decay_placement(hf, hk)
    tri = lambda n: jnp.asarray(np.tril(np.ones((n, n), np.float32)), BF)
    seq_tri = jnp.asarray(np.kron(np.eye(tm_s // Ts, dtype=np.float32), np.tril(np.ones((Ts, Ts), np.float32))), BF)

    tab_p = _rope_tables(jnp.arange(T, dtype=jnp.int32), rope)
    x = x_prompt
    p_state = []
    for l in range(L):
        W = layers[l]
        mod = mod_all[l, 0:1][:, None, :]
        x = _layer_dense_pre(x, mod, W, tm_p)
        (c_new, kr_new, fk, fv, logf, oconv, utail, qm, km, vm, qf, kf, vf) = _mix_in_call(
            x, mod, W, tab_p, True, tm_in, dict(ltri=tri(tm_in), pq=pq, pk=pk))
        o_mla = _flash_call(qm, km, vm, tq)
        o_fox = _flash_call(qf, kf, vf, tq)
        x = _layer_dense_post(x, mod, W, o_mla, o_fox, oconv, tm_p)
        p_state.append((c_new.reshape(1, T, lat), kr_new.reshape(1, T, rope),
                        fk.reshape(1, T, hk, hd), fv.reshape(1, T, hk, hd), logf.reshape(1, T, hf),
                        utail[-1, SUBLANE - 2:, :][None]))
    y_prompt = _final_call(x.reshape(T, D), g_final[None, :], tm_p).reshape(1, T, D)

    pt3 = page_table.reshape(B, 1, n_pages)
    caches = (cache_mla_latent, jnp.transpose(cache_mla_krope, (0, 1, 3, 2)),
              jnp.transpose(cache_fox_k, (0, 1, 3, 4, 2)), jnp.transpose(cache_fox_v, (0, 1, 3, 4, 2)),
              jnp.transpose(cache_fox_logf, (0, 1, 3, 2)))
    tab_s = _rope_tables(past + (jnp.arange(Ns, dtype=jnp.int32) % Ts), rope)
    x = x_sample
    s_state = []
    for l in range(L):
        W = layers[l]
        mod = mod_all[l, SUBLANE:][:, None, :]
        x = _layer_dense_pre(x, mod, W, tm_s)
        st = state_conv[l]
        zero = jnp.zeros((B, Ts - 2, cw), F32)
        halo_a = jnp.concatenate([st, zero], axis=1).reshape(Ns, cw)
        halo_b = jnp.concatenate([st[:, 1:2], jnp.zeros((B, Ts - 1, cw), F32)], axis=1).reshape(Ns, cw)
        (c_new, kr_new, fk, fv, logf, oconv, u_all, qm, qlat, qf, cum) = _mix_in_call(
            x, mod, W, tab_s, False, tm_s, dict(ltri=seq_tri, halo_a=halo_a, halo_b=halo_b))

        def head_major(a):
            w = a.shape[1] // H
            return a.reshape(B, Ts, H, w).transpose(0, 2, 1, 3).reshape(B, H * Ts, w)

        fq_hm = head_major(qf)[:, :, :hd].reshape(B, hf, Ts, hd)
        zq_ = jnp.zeros_like(fq_hm[:, :hf // hk])
        fq_dec = jnp.concatenate([jnp.concatenate([fq_hm[:, :hf // hk], zq_], axis=-1),
                                  jnp.concatenate([zq_, fq_hm[:, hf // hk:]], axis=-1)], axis=1)
        cum_new = cum[:, :hf].reshape(B, Ts, hf)
        cum_hm = cum_new.transpose(0, 2, 1)
        pad_keys = lambda a: jnp.pad(a.reshape(B, Ts, -1).astype(BF), ((0, 0), (0, page - Ts), (0, 0)))
        dq = dict(
            qlat=head_major(qlat), qrope=head_major(qm)[:, :, :rope],
            fq=fq_dec.reshape(B, hf * Ts, LANE),
            cumb=jnp.broadcast_to(cum_hm.reshape(B, hf * Ts, 1), (B, hf * Ts, LANE)),
            cn=pad_keys(c_new), krn=pad_keys(kr_new), fkn=pad_keys(fk), fvn=pad_keys(fv),
            rtn=jnp.pad(-cum_hm, ((0, 0), (0, 0), (0, page - Ts))),
        )
        o_mla, o_fox = _decode_call(l, pt3, dq, caches, W["wuv_pad"], chunk)
        x = _layer_dense_post(x, mod, W, o_mla.reshape(Ns, H * LANE), o_fox.reshape(Ns, hf * LANE), oconv, tm_s)
        s_state.append((c_new.reshape(B, Ts, lat), kr_new.reshape(B, Ts, rope),
                        fk.reshape(B, Ts, hk, hd), fv.reshape(B, Ts, hk, hd), logf.reshape(B, Ts, hf),
                        u_all.reshape(B, Ts, cw)[:, Ts - 2:, :]))
    y_sample = _final_call(x.reshape(Ns, D), g_final[None, :], tm_s).reshape(B, Ts, D)

    stack = lambda states, i: jnp.stack([s[i] for s in states], axis=0)
    return ((y_prompt, y_sample) + tuple(stack(p_state, i) for i in range(6))
            + tuple(stack(s_state, i) for i in range(6)))
```

```python
import functools

import numpy as np
import jax
import jax.numpy as jnp
from jax import lax
from jax.experimental import pallas as pl
from jax.experimental.pallas import tpu as pltpu

F32 = jnp.float32
BF = jnp.bfloat16

RMS_EPS = 1e-6
ROPE_THETA = 10000.0
LANE = 128
SUBLANE = 8
VMEM_LIMIT = 56 * 1024 * 1024

NT_DIMS = (((1,), (1,)), ((), ()))
LOG2E = 1.4426950408889634
ONES_ROW = LANE // 2


def _dot(a, b):
    return jnp.dot(a, b, preferred_element_type=F32)


def _dot_nt(a, b):
    return lax.dot_general(a, b, NT_DIMS, preferred_element_type=F32)


def _split3(x):
    hi = x.astype(BF)
    r1 = x - hi.astype(F32)
    mid = r1.astype(BF)
    lo = (r1 - mid.astype(F32)).astype(BF)
    return hi, mid, lo


def _dot_exact_rhs(a_bf, x):
    hi, mid, lo = _split3(x)
    return _dot(a_bf, hi) + _dot(a_bf, mid) + _dot(a_bf, lo)


def _sigmoid(x):
    return 1.0 / (1.0 + jnp.exp(-x))


def _rms(x, g):
    ms = jnp.mean(x * x, axis=-1, keepdims=True)
    return x * lax.rsqrt(ms + RMS_EPS) * g


def _const_spec(shape):
    nd = len(shape)
    return pl.BlockSpec(shape, lambda *_: (0,) * nd, pipeline_mode=pl.Buffered(1))


def _params(sem):
    return pltpu.CompilerParams(dimension_semantics=sem, vmem_limit_bytes=VMEM_LIMIT)


def _row_blocking(S, R, tm):
    if R >= tm:
        assert R % tm == 0
        return 1, tm
    assert tm % R == 0 and S % (tm // R) == 0
    return tm // R, R


def _x_spec(S, R, D, bs, rb):
    per_seq = R // rb
    return pl.BlockSpec((bs, rb, D), lambda i: (i // per_seq, i % per_seq, 0))


def _mod_spec(D, bs, per_seq, k):
    return pl.BlockSpec((bs, 1, D), lambda i: (i // per_seq, 0, k))


def _rows_spec(width, tm):
    return pl.BlockSpec((tm, width), lambda i: (i, 0))


def _modulated_norm(x3, g_ref, sc_ref, sh_ref):
    bs, rb, D = x3.shape
    h3 = _rms(x3, g_ref[...].reshape(1, 1, D)) * (1.0 + sc_ref[...]) + sh_ref[...]
    return h3.reshape(bs * rb, D).astype(BF)


def _mod_kernel(c_ref, w_ref, b_ref, o_ref):
    c = c_ref[...]
    a = (c * _sigmoid(c)).astype(BF)
    o_ref[0] = _dot(a, w_ref[0]) + b_ref[0]


def _mod_call(c_rows, w_ada, b_ada):
    L, D, M = w_ada.shape
    Rc = c_rows.shape[0]
    tn = D
    return pl.pallas_call(
        _mod_kernel,
        grid=(L, M // tn),
        in_specs=[pl.BlockSpec((Rc, D), lambda l, j: (0, 0)),
                  pl.BlockSpec((1, D, tn), lambda l, j: (l, 0, j)),
                  pl.BlockSpec((1, 1, tn), lambda l, j: (l, 0, j))],
        out_specs=pl.BlockSpec((1, Rc, tn), lambda l, j: (l, 0, j)),
        out_shape=jax.ShapeDtypeStruct((L, Rc, M), F32),
        compiler_params=_params(("parallel", "parallel")),
        name="ada_mod",
    )(c_rows, w_ada, b_ada.reshape(L, 1, M))


def _ffn_kernel(x_ref, sh_ref, sc_ref, gt_ref, g_ref, wg_ref, wu_ref, wo_ref, o_ref, *, n_chunks):
    x3 = x_ref[...]
    bs, rb, D = x3.shape
    h = _modulated_norm(x3, g_ref, sc_ref, sh_ref)
    fc = wg_ref.shape[1] // n_chunks
    acc = jnp.zeros((bs * rb, D), F32)
    for c in range(n_chunks):
        g = _dot(h, wg_ref[:, c * fc:(c + 1) * fc])
        u = _dot(h, wu_ref[:, c * fc:(c + 1) * fc])
        a = (g * _sigmoid(g) * u).astype(BF)
        acc = acc + _dot(a, wo_ref[c * fc:(c + 1) * fc, :])
    o_ref[...] = x3 + 0.5 * gt_ref[...] * acc.reshape(bs, rb, D)


def _ffn_call(x, mod, k0, g, wg, wu, wo, tm):
    S, R, D = x.shape
    F = wg.shape[1]
    bs, rb = _row_blocking(S, R, tm)
    per_seq = R // rb
    n_chunks = 2 if F % (2 * LANE) == 0 else 1
    return pl.pallas_call(
        functools.partial(_ffn_kernel, n_chunks=n_chunks),
        grid=(S * R // tm,),
        in_specs=[_x_spec(S, R, D, bs, rb),
                  _mod_spec(D, bs, per_seq, k0), _mod_spec(D, bs, per_seq, k0 + 1),
                  _mod_spec(D, bs, per_seq, k0 + 2),
                  _const_spec((1, D)), _const_spec((D, F)), _const_spec((D, F)), _const_spec((F, D))],
        out_specs=_x_spec(S, R, D, bs, rb),
        out_shape=jax.ShapeDtypeStruct((S, R, D), F32),
        compiler_params=_params(("parallel",)),
        name="ffn",
    )(x, mod, mod, mod, g, wg, wu, wo)


class _Seg:
    def __init__(self, q_rank, kv_rank, hq, conv):
        widths = dict(zq=q_rank, zkv=kv_rank, misc=3 * LANE, fq=hq * LANE, fkv=2 * LANE,
                      fkp=2 * LANE, conv=3 * conv)
        self.off = {}
        o = 0
        for name, w in widths.items():
            self.off[name] = (o, o + w)
            o += w
        self.total = o


def _with_ones_rows(vt):
    row = lax.broadcasted_iota(jnp.int32, vt.shape, 0) % LANE
    return jnp.where(row == ONES_ROW, 1.0, vt)


def _mix_in_kernel(*refs, seg, heads, prompt, scale):
    it = iter(refs)
    x_ref, sh_ref, sc_ref, gmix_ref, w_ref = (next(it) for _ in range(5))
    gq_ref, wuqa_ref, wuqb_ref, gkv_ref = (next(it) for _ in range(4))
    if prompt:
        wk_ref, wvt_ref, wfvt_ref = next(it), next(it), next(it)
    else:
        wukt_ref = next(it)
    bf_ref, wconv_ref, cos_ref, sin_ref, ltri_ref = (next(it) for _ in range(5))
    if prompt:
        pq_ref, pk_ref = next(it), next(it)
    else:
        halo_a_ref, halo_b_ref = next(it), next(it)
    c_out, kr_out, fk_out, fv_out, lf_out, oconv_out, utail_out, qm_out = (next(it) for _ in range(8))
    if prompt:
        km_out, vm_out, qf_out, kf_out, vf_out = (next(it) for _ in range(5))
    else:
        qlat_out, qf_out, cum_out = (next(it) for _ in range(3))
    ubuf = next(it)
    if prompt:
        cum_carry = next(it)

    i = pl.program_id(0)
    x3 = x_ref[...]
    h = _modulated_norm(x3, gmix_ref, sc_ref, sh_ref)
    tm = h.shape[0]

    def proj(name):
        a, b = seg.off[name]
        return _dot(h, w_ref[:, a:b])

    cos = cos_ref[...]
    sin = sin_ref[...]

    cq = _rms(proj("zq"), gq_ref[...]).astype(BF)
    qa = _dot(cq, wuqa_ref[...])
    qb = _dot(cq, wuqb_ref[...])
    q_heads = []
    for hh in range(heads):
        sl = slice(hh * LANE, (hh + 1) * LANE)
        qh = ((qa[:, sl] * cos + qb[:, sl] * sin) * scale).astype(BF)
        qm_out[:, sl] = qh
        q_heads.append(qh)

    c_new = _rms(proj("zkv"), gkv_ref[...])
    c_out[...] = c_new
    cb = c_new.astype(BF)
    zm = proj("misc")
    kr = zm[:, 0:LANE] * cos + zm[:, LANE:2 * LANE] * sin
    kr_out[...] = kr[:, 0:kr_out.shape[1]]
    if prompt:
        knope = _dot(cb, wk_ref[...])
        for hh in range(heads):
            sl = slice(hh * LANE, (hh + 1) * LANE)
            km_out[:, sl] = (knope[:, sl] + kr).astype(BF)
        vm_out[0] = _with_ones_rows(_dot_nt(wvt_ref[...], cb)).astype(BF)
    else:
        lat = qlat_out.shape[1] // heads
        for hh in range(heads):
            qlat_out[:, hh * lat:(hh + 1) * lat] = _dot(q_heads[hh], wukt_ref[hh]).astype(BF)

    zff = zm[:, 2 * LANE:3 * LANE] + bf_ref[...]
    logf = jnp.minimum(zff, 0.0) - jnp.log1p(jnp.exp(-jnp.abs(zff)))
    lf_out[...] = logf[:, 0:lf_out.shape[1]]
    zfkv = proj("fkv")
    fk_out[...] = zfkv[:, 0:LANE]
    fv_out[...] = zfkv[:, LANE:2 * LANE]
    cum = _dot_exact_rhs(ltri_ref[...], logf)
    zfq = proj("fq")
    if prompt:
        @pl.when(i == 0)
        def _():
            cum_carry[...] = jnp.zeros_like(cum_carry)
        cum = cum + cum_carry[...]
        cum_carry[...] = cum[tm - 1:tm, :]
        zfq = zfq * LOG2E
        hi, mid, lo = _split3(cum * LOG2E)
        lane = lax.broadcasted_iota(jnp.int32, hi.shape, 1)
        hi = jnp.where(lane == heads, jnp.ones_like(hi), hi)
        sq = jnp.concatenate([hi, mid, lo], axis=1)
        qf_out[...] = (zfq + _dot(sq, pq_ref[...])).astype(BF)
        kf_out[...] = (proj("fkp") + _dot(sq, pk_ref[...])).astype(BF)
        vf_out[0] = _with_ones_rows(_dot_nt(wfvt_ref[...], h)).astype(BF)
    else:
        qf_out[...] = zfq.astype(BF)
        cum_out[...] = cum

    zc = proj("conv")
    cw = zc.shape[1] // 3
    zb, u = zc[:, 0:cw], zc[:, cw:2 * cw] * zc[:, 2 * cw:3 * cw]
    @pl.when(i == 0)
    def _():
        ubuf[0:SUBLANE, :] = jnp.zeros((SUBLANE, cw), F32)
    ubuf[SUBLANE:SUBLANE + tm, :] = u
    um1 = ubuf[SUBLANE - 1:SUBLANE - 1 + tm, :]
    um2 = ubuf[SUBLANE - 2:SUBLANE - 2 + tm, :]
    if not prompt:
        r = lax.broadcasted_iota(jnp.int32, u.shape, 0) % SUBLANE
        um2 = jnp.where(r < 2, halo_a_ref[...], um2)
        um1 = jnp.where(r == 0, halo_b_ref[...], um1)
    wc = wconv_ref[...]
    y = wc[0:1, :] * um2 + wc[1:2, :] * um1 + wc[2:3, :] * u
    oconv_out[...] = (zb * y).astype(BF)
    if prompt:
        utail_out[0] = u[tm - SUBLANE:tm, :]
        ubuf[0:SUBLANE, :] = u[tm - SUBLANE:tm, :]
    else:
        utail_out[...] = u


def _mix_in_call(x, mod, wts, tables, prompt, tm, extra):
    S, R, D = x.shape
    N = S * R
    bs, rb = _row_blocking(S, R, tm)
    per_seq = R // rb
    seg, heads = wts["seg"], wts["heads"]
    hq = wts["fox_heads"]
    nblk = N // tm
    kr_w, lf_w, cw = wts["rope"], wts["fox_heads"], wts["conv"]
    in_arrays = [x, mod, mod, wts["g_mix"], wts["w_in"], wts["g_q"], wts["wuq_a"], wts["wuq_b"], wts["g_kv"]]
    in_specs = [_x_spec(S, R, D, bs, rb), _mod_spec(D, bs, per_seq, 3), _mod_spec(D, bs, per_seq, 4),
                _const_spec((1, D)), _const_spec(wts["w_in"].shape), _const_spec(wts["g_q"].shape),
                _const_spec(wts["wuq_a"].shape), _const_spec(wts["wuq_b"].shape), _const_spec(wts["g_kv"].shape)]
    if prompt:
        in_arrays += [wts["wk_pad"], wts["wvt_pad"], wts["wfvt_pad"]]
        in_specs += [_const_spec(wts["wk_pad"].shape), _const_spec(wts["wvt_pad"].shape),
                     _const_spec(wts["wfvt_pad"].shape)]
    else:
        in_arrays += [wts["wukt_pad"]]
        in_specs += [_const_spec(wts["wukt_pad"].shape)]
    in_arrays += [wts["b_forget"], wts["w_conv"], tables["cos"], tables["sin"], extra["ltri"]]
    in_specs += [_const_spec(wts["b_forget"].shape), _const_spec(wts["w_conv"].shape),
                 _rows_spec(LANE, tm), _rows_spec(LANE, tm), _const_spec((tm, tm))]
    if prompt:
        in_arrays += [extra["pq"], extra["pk"]]
        in_specs += [_const_spec(extra["pq"].shape), _const_spec(extra["pk"].shape)]
    else:
        in_arrays += [extra["halo_a"], extra["halo_b"]]
        in_specs += [_rows_spec(cw, tm), _rows_spec(cw, tm)]

    lat = wts["kv_rank"]
    out = [((N, lat), F32), ((N, kr_w), F32), ((N, LANE), F32), ((N, LANE), F32), ((N, lf_w), F32),
           ((N, cw), BF)]
    out_specs = [_rows_spec(lat, tm), _rows_spec(kr_w, tm), _rows_spec(LANE, tm), _rows_spec(LANE, tm),
                 _rows_spec(lf_w, tm), _rows_spec(cw, tm)]
    if prompt:
        out += [((nblk, SUBLANE, cw), F32)]
        out_specs += [pl.BlockSpec((1, SUBLANE, cw), lambda i: (i, 0, 0))]
    else:
        out += [((N, cw), F32)]
        out_specs += [_rows_spec(cw, tm)]
    out += [((N, heads * LANE), BF)]
    out_specs += [_rows_spec(heads * LANE, tm)]
    if prompt:
        tile_t = lambda rows: pl.BlockSpec((1, rows, tm), lambda i: (i, 0, 0))
        out += [((N, heads * LANE), BF), ((nblk, heads * LANE, tm), BF), ((N, hq * LANE), BF),
                ((N, 2 * LANE), BF), ((nblk, 2 * LANE, tm), BF)]
        out_specs += [_rows_spec(heads * LANE, tm), tile_t(heads * LANE), _rows_spec(hq * LANE, tm),
                      _rows_spec(2 * LANE, tm), tile_t(2 * LANE)]
    else:
        out += [((N, heads * lat), BF), ((N, hq * LANE), BF), ((N, LANE), F32)]
        out_specs += [_rows_spec(heads * lat, tm), _rows_spec(hq * LANE, tm), _rows_spec(LANE, tm)]
    scratch = [pltpu.VMEM((tm + SUBLANE, cw), F32)]
    if prompt:
        scratch += [pltpu.VMEM((1, LANE), F32)]
    return pl.pallas_call(
        functools.partial(_mix_in_kernel, seg=seg, heads=heads, prompt=prompt,
                          scale=wts["mla_scale"] * (LOG2E if prompt else 1.0)),
        grid=(nblk,),
        in_specs=in_specs,
        out_specs=out_specs,
        out_shape=[jax.ShapeDtypeStruct(s, d) for s, d in out],
        scratch_shapes=scratch,
        compiler_params=_params(("arbitrary",)),
        name="mix_in_prompt" if prompt else "mix_in_sample",
    )(*in_arrays)


FLASH_HEADS = 4


def _flash_kernel(q_ref, k_ref, vt_ref, o_ref, s_scr, p_scr, m_scr, alpha_scr, acc_scr, *, shared_kv):
    i = pl.program_id(1)
    tq = q_ref.shape[0]
    tk = vt_ref.shape[2]
    nh = q_ref.shape[1] // LANE
    heads = range(nh)
    qs = [q_ref[:, a * LANE:(a + 1) * LANE] for a in heads]
    kv = [slice(0, LANE) if shared_kv else slice(a * LANE, (a + 1) * LANE) for a in heads]

    def scores(j, slot):
        start = pl.multiple_of(j * tk, tk)
        for a in heads:
            s_scr[slot, a] = _dot_nt(k_ref[pl.ds(start, tk), kv[a]], qs[a])

    def values(j, slot):
        for a in heads:
            acc_scr[a] = alpha_scr[slot, a] * acc_scr[a] + _dot(vt_ref[j, kv[a], :], p_scr[slot, a])

    def step(j, slot, masked, last):
        rows_per_pass = 64
        if not last:
            scores(j + 1, 1 - slot)
        values(jnp.maximum(j - 1, 0), 1 - slot)

        def masked_scores(a, rows, cols):
            sa = s_scr[slot, a, rows, cols]
            if masked:
                key = j * tk + rows.start + lax.broadcasted_iota(jnp.int32, sa.shape, 0)
                qry = i * tq + cols.start + lax.broadcasted_iota(jnp.int32, sa.shape, 1)
                sa = jnp.where(key <= qry, sa, -jnp.inf)
            return sa

        m_new = []
        for a in heads:
            m_old = m_scr[a]
            strips = [jnp.max(masked_scores(a, slice(0, tk), slice(c * LANE, (c + 1) * LANE)), axis=0, keepdims=True)
                      for c in range(tq // LANE)]
            mn = jnp.maximum(m_old, jnp.concatenate(strips, axis=1))
            m_scr[a] = mn
            m_new.append(mn)
            alpha_scr[slot, a] = jnp.exp2(m_old - mn)
        for a in heads:
            for r in range(tk // rows_per_pass):
                rows = slice(r * rows_per_pass, (r + 1) * rows_per_pass)
                p_scr[slot, a, rows, :] = jnp.exp2(masked_scores(a, rows, slice(0, tq)) - m_new[a]).astype(BF)

    n_diag = tq // tk
    assert n_diag == 2
    m_scr[...] = jnp.full(m_scr.shape, -jnp.inf, F32)
    acc_scr[...] = jnp.zeros(acc_scr.shape, F32)
    p_scr[1] = jnp.zeros(p_scr.shape[1:], BF)
    alpha_scr[1] = jnp.ones(alpha_scr.shape[1:], F32)
    scores(0, 0)

    def body(jj, _):
        step(2 * jj, 0, False, False)
        step(2 * jj + 1, 1, False, False)
        return 0

    lax.fori_loop(0, i, body, 0)
    for d in range(n_diag):
        step(2 * i + d, d, True, d == n_diag - 1)
    values(2 * i + n_diag - 1, (n_diag - 1) % 2)
    for a in heads:
        acc = acc_scr[a]
        o_ref[:, a * LANE:(a + 1) * LANE] = (acc / acc[ONES_ROW:ONES_ROW + 1, :]).T.astype(o_ref.dtype)


def _flash_call(q, k, vt, tq):
    N, qw = q.shape
    nk, _, tk = vt.shape
    hq, hk = qw // LANE, k.shape[1] // LANE
    nh = min(FLASH_HEADS, hq)
    shared_kv = hk != hq
    assert tq % tk == 0 and nk * tk == N and hq % nh == 0
    assert (not shared_kv) or hq // hk == nh
    kvw = LANE if shared_kv else nh * LANE
    return pl.pallas_call(
        functools.partial(_flash_kernel, shared_kv=shared_kv),
        grid=(hq // nh, N // tq),
        in_specs=[pl.BlockSpec((tq, nh * LANE), lambda g, i: (i, g)),
                  pl.BlockSpec((N, kvw), lambda g, i: (0, g), pipeline_mode=pl.Buffered(1)),
                  pl.BlockSpec((nk, kvw, tk), lambda g, i: (0, g, 0), pipeline_mode=pl.Buffered(1))],
        out_specs=pl.BlockSpec((tq, nh * LANE), lambda g, i: (i, g)),
        out_shape=jax.ShapeDtypeStruct((N, qw), BF),
        scratch_shapes=[pltpu.VMEM((2, nh, tk, tq), F32), pltpu.VMEM((2, nh, tk, tq), BF),
                        pltpu.VMEM((nh, 1, tq), F32), pltpu.VMEM((2, nh, 1, tq), F32),
                        pltpu.VMEM((nh, LANE, tq), F32)],
        compiler_params=_params(("parallel", "arbitrary")),
        name="flash_prompt",
    )(q, k, vt)


def _mix_out_kernel(x_ref, sh_ref, sc_ref, gt_ref, gmix_ref, wg_ref, bg_ref, oa_ref, ob_ref, oc_ref,
                    wpa_ref, wpb_ref, wpc_ref, wo_ref, o_ref):
    x3 = x_ref[...]
    bs, rb, D = x3.shape
    h = _modulated_norm(x3, gmix_ref, sc_ref, sh_ref)
    m = jnp.zeros((bs * rb, D), F32)
    for b, (o_r, w_r) in enumerate(((oa_ref, wpa_ref), (ob_ref, wpb_ref), (oc_ref, wpc_ref))):
        zg = _dot(h, wg_ref[:, b * D:(b + 1) * D]) + bg_ref[:, b * D:(b + 1) * D]
        m = m + _sigmoid(zg) * _dot(o_r[...], w_r[...])
    out = _dot(m.astype(BF), wo_ref[...])
    o_ref[...] = x3 + gt_ref[...] * out.reshape(bs, rb, D)


def _mix_out_call(x, mod, wts, oa, ob, oc, tm):
    S, R, D = x.shape
    bs, rb = _row_blocking(S, R, tm)
    per_seq = R // rb
    names = ("g_mix", "w_gate", "b_gate")
    wnames = ("w_pa", "w_pb", "w_pc", "w_o")
    return pl.pallas_call(
        _mix_out_kernel,
        grid=(S * R // tm,),
        in_specs=[_x_spec(S, R, D, bs, rb), _mod_spec(D, bs, per_seq, 3), _mod_spec(D, bs, per_seq, 4),
                  _mod_spec(D, bs, per_seq, 5)]
                 + [_const_spec(wts[n].shape) for n in names]
                 + [_rows_spec(oa.shape[1], tm), _rows_spec(ob.shape[1], tm), _rows_spec(oc.shape[1], tm)]
                 + [_const_spec(wts[n].shape) for n in wnames],
        out_specs=_x_spec(S, R, D, bs, rb),
        out_shape=jax.ShapeDtypeStruct((S, R, D), F32),
        compiler_params=_params(("parallel",)),
        name="mix_out",
    )(x, mod, mod, mod, *[wts[n] for n in names], oa, ob, oc, *[wts[n] for n in wnames])


def _final_kernel(x_ref, g_ref, o_ref):
    o_ref[...] = _rms(x_ref[...], g_ref[...])


def _final_call(x2, g, tm):
    N, D = x2.shape
    return pl.pallas_call(
        _final_kernel,
        grid=(N // tm,),
        in_specs=[_rows_spec(D, tm), _const_spec((1, D))],
        out_specs=_rows_spec(D, tm),
        out_shape=jax.ShapeDtypeStruct((N, D), F32),
        compiler_params=_params(("parallel",)),
        name="final_norm",
    )(x2, g)


def _decode_kernel(pt_cur, pt_next, qlat_ref, qrope_ref, fq_ref, cumb_ref,
                   cn_ref, krn_ref, fkn_ref, fvn_ref, rtn_ref, wuv_ref, usum_ref,
                   lat_hbm, krt_hbm, fkt_hbm, fvt_hbm, lft_hbm,
                   om_ref, of_ref,
                   lat_buf, krt_buf, fkt_buf, fvt_buf, lf_buf, sems, lf_sems,
                   within_scr, total_scr, st_m, st_f, acc_m, acc_f, *, layer, chunk, n_chunks, heads, tokens):
    b = pl.program_id(0)
    nb = pl.num_programs(0)
    caches = (lat_hbm, krt_hbm, fkt_hbm, fvt_hbm)
    bufs = (lat_buf, krt_buf, fkt_buf, fvt_buf)
    n_pages = chunk * n_chunks
    page = lat_buf.shape[2]
    keys = chunk * page
    last = n_chunks - 1
    lf_slot = b % 2

    def page_copy(a, pg, slot, p):
        return pltpu.make_async_copy(caches[a].at[layer, pg], bufs[a].at[slot, p], sems.at[slot, a])

    def lf_copy(pg, slot, p):
        return pltpu.make_async_copy(lft_hbm.at[layer, pg], lf_buf.at[slot, p], lf_sems.at[slot])

    def start_chunk(page_of, slot):
        for p in range(chunk):
            pg = page_of(p)
            for a in range(len(caches)):
                page_copy(a, pg, slot, p).start()

    def wait_chunk(slot):
        for p in range(chunk):
            for a in range(len(caches)):
                page_copy(a, 0, slot, p).wait()

    def start_lf(pt_ref, slot):
        for p in range(n_pages):
            lf_copy(pt_ref[0, 0, p], slot, p).start()

    def wait_lf(slot):
        for p in range(n_pages):
            lf_copy(0, slot, p).wait()

    @pl.when(b == 0)
    def _():
        start_chunk(lambda p: pt_cur[0, 0, last * chunk + p], 0)
        start_lf(pt_cur, 0)

    start_lf(pt_next, 1 - lf_slot)
    wait_lf(lf_slot)
    hi, mid, lo = _split3(lf_buf[lf_slot].reshape(n_pages * heads, LANE))
    usum = usum_ref[...]
    y = _dot(hi, usum) + _dot(mid, usum) + _dot(lo, usum)
    within_scr[...] = y[:, 0:LANE]
    total_scr[...] = y[:, LANE:2 * LANE]

    qlat = qlat_ref[0]
    qrope = qrope_ref[0]
    fq = fq_ref[0]
    cumb = cumb_ref[0]
    rows = qlat.shape[0]

    st_m[0] = jnp.full((rows, LANE), -jnp.inf, F32)
    st_m[1] = jnp.zeros((rows, LANE), F32)
    st_f[0] = jnp.full((rows, LANE), -jnp.inf, F32)
    st_f[1] = jnp.zeros((rows, LANE), F32)
    acc_m[...] = jnp.zeros_like(acc_m)
    acc_f[...] = jnp.zeros_like(acc_f)

    def expand_heads(rt):
        w = rt.shape[1]
        return jnp.concatenate([jnp.broadcast_to(rt[hh:hh + 1, :], (tokens, w)) for hh in range(heads)], axis=0)

    def soften(st, s):
        m_old = st[0][:, 0:1]
        l_old = st[1][:, 0:1]
        m_new = jnp.maximum(m_old, jnp.max(s, axis=-1, keepdims=True))
        alpha = jnp.exp(m_old - m_new)
        p = jnp.exp(s - m_new)
        l_new = alpha * l_old + jnp.sum(p, axis=-1, keepdims=True)
        st[0] = jnp.broadcast_to(m_new, (rows, LANE))
        st[1] = jnp.broadcast_to(l_new, (rows, LANE))
        return p.astype(BF), alpha

    def body(k, slot, carry):
        c = last - k
        wrap = k == last
        c_next = jnp.where(wrap, last, c - 1)

        def next_page(p):
            idx = c_next * chunk + p
            return jnp.where(wrap, pt_next[0, 0, idx], pt_cur[0, 0, idx])

        start_chunk(next_page, 1 - slot)
        wait_chunk(slot)
        lat = lat_buf[slot].reshape(keys, lat_buf.shape[3]).astype(BF)
        krt = jnp.concatenate([krt_buf[slot, p] for p in range(chunk)], axis=1).astype(BF)
        fkt = jnp.concatenate([fkt_buf[slot, p].reshape(LANE, page) for p in range(chunk)], axis=1).astype(BF)
        fvt = jnp.concatenate([fvt_buf[slot, p].reshape(LANE, page) for p in range(chunk)], axis=1).astype(BF)
        rts = [None] * chunk
        for p in reversed(range(chunk)):
            row0 = pl.multiple_of((c * chunk + p) * heads, heads)
            rts[p] = within_scr[pl.ds(row0, heads), :] + carry
            carry = carry + total_scr[pl.ds(row0, heads), :]
        s_m = _dot_nt(qlat, lat) + _dot(qrope, krt)
        s_f = (_dot(fq, fkt) + expand_heads(jnp.concatenate(rts, axis=1))
               + jnp.concatenate([cumb] * chunk, axis=1))
        p_m, alpha_m = soften(st_m, s_m)
        p_f, alpha_f = soften(st_f, s_f)
        acc_m[...] = alpha_m * acc_m[...] + _dot(p_m, lat)
        acc_f[...] = alpha_f * acc_f[...] + _dot_nt(p_f, fvt)
        return carry

    def two_steps(kk, carry):
        return body(2 * kk + 1, 1, body(2 * kk, 0, carry))

    lax.fori_loop(0, n_chunks // 2, two_steps, jnp.zeros((heads, LANE), F32))

    @pl.when(b == nb - 1)
    def _():
        wait_chunk((last + 1) % 2)
        wait_lf(1 - lf_slot)

    r = lax.broadcasted_iota(jnp.int32, (rows, page), 0) % tokens
    col = lax.broadcasted_iota(jnp.int32, (rows, page), 1)
    valid = col <= r
    cn = cn_ref[0]
    fvn = fvn_ref[0]
    s_m = _dot_nt(qlat, cn) + _dot_nt(qrope, krn_ref[0])
    s_f = _dot_nt(fq, fkn_ref[0]) + expand_heads(rtn_ref[0]) + cumb
    p_m, alpha_m = soften(st_m, jnp.where(valid, s_m, -jnp.inf))
    p_f, alpha_f = soften(st_f, jnp.where(valid, s_f, -jnp.inf))
    acc_m[...] = alpha_m * acc_m[...] + _dot(p_m, cn)
    acc_f[...] = alpha_f * acc_f[...] + _dot(p_f, fvn)

    o_lat = (acc_m[...] / st_m[1][:, 0:1]).astype(BF)
    o_fox = acc_f[...] / st_f[1][:, 0:1]
    half = LANE // 2
    lane = lax.broadcasted_iota(jnp.int32, (tokens, LANE), 1)
    group = heads // 2
    for hh in range(heads):
        rs = slice(hh * tokens, (hh + 1) * tokens)
        om_ref[0, :, hh * LANE:(hh + 1) * LANE] = _dot(o_lat[rs, :], wuv_ref[hh]).astype(BF)
        of = o_fox[rs, :]
        if hh >= group:
            of = pltpu.roll(of, half, axis=1)
        of_ref[0, :, hh * LANE:(hh + 1) * LANE] = jnp.where(lane < half, of, 0.0).astype(BF)


def _decode_call(layer, pt3, dq, caches, wuv_pad, chunk):
    B, _, n_pages = pt3.shape
    lat_c, krt_c, fkt_c, fvt_c, lft_c = caches
    page = lat_c.shape[2]
    heads = wuv_pad.shape[0]
    rows = dq["qlat"].shape[1]
    tokens = rows // heads
    assert n_pages % chunk == 0 and (n_pages // chunk) % 2 == 0
    assert lft_c.shape[2] == heads == SUBLANE and page == LANE and fkt_c.shape[2] * fkt_c.shape[3] == LANE
    later = np.arange(page)[:, None] > np.arange(page)[None, :]
    usum = jnp.asarray(np.concatenate([later, np.ones((page, page), bool)], axis=1), BF)
    per_seq = lambda shape: pl.BlockSpec((1,) + shape, lambda b: (b,) + (0,) * len(shape))
    names = ("qlat", "qrope", "fq", "cumb", "cn", "krn", "fkn", "fvn", "rtn")
    in_specs = ([pl.BlockSpec((1, 1, n_pages), lambda b: (b, 0, 0), memory_space=pltpu.SMEM),
                 pl.BlockSpec((1, 1, n_pages), lambda b: ((b + 1) % B, 0, 0), memory_space=pltpu.SMEM)]
                + [per_seq(dq[n].shape[1:]) for n in names]
                + [_const_spec(wuv_pad.shape), _const_spec(usum.shape)]
                + [pl.BlockSpec(memory_space=pl.ANY)] * len(caches))
    ow = heads * LANE
    return pl.pallas_call(
        functools.partial(_decode_kernel, layer=layer, chunk=chunk, n_chunks=n_pages // chunk,
                          heads=heads, tokens=tokens),
        grid=(B,),
        in_specs=in_specs,
        out_specs=[per_seq((tokens, ow)), per_seq((tokens, ow))],
        out_shape=[jax.ShapeDtypeStruct((B, tokens, ow), BF)] * 2,
        scratch_shapes=[pltpu.VMEM((2, chunk) + lat_c.shape[2:], F32),
                        pltpu.VMEM((2, chunk) + krt_c.shape[2:], F32),
                        pltpu.VMEM((2, chunk) + fkt_c.shape[2:], F32),
                        pltpu.VMEM((2, chunk) + fvt_c.shape[2:], F32),
                        pltpu.VMEM((2, n_pages) + lft_c.shape[2:], F32),
                        pltpu.SemaphoreType.DMA((2, 4)), pltpu.SemaphoreType.DMA((2,)),
                        pltpu.VMEM((n_pages * heads, LANE), F32), pltpu.VMEM((n_pages * heads, LANE), F32),
                        pltpu.VMEM((2, rows, LANE), F32), pltpu.VMEM((2, rows, LANE), F32),
                        pltpu.VMEM((rows, lat_c.shape[3]), F32), pltpu.VMEM((rows, LANE), F32)],
        compiler_params=_params(("arbitrary",)),
        name="paged_decode",
    )(pt3, pt3, *[dq[n] for n in names], wuv_pad, usum, *caches)


def _pad_cols(w, width):
    return jnp.pad(w, ((0, 0), (0, width - w.shape[1])))


def _rot_half_cols(w):
    half = w.shape[1] // 2
    return jnp.concatenate([-w[:, half:], w[:, :half]], axis=1)


def _pack_layer(l, P, dims):
    D, H, nope, rope, lat = dims["D"], dims["H"], dims["nope"], dims["rope"], dims["lat"]
    hf, hk, hd, cw = dims["HF"], dims["HK"], dims["HD"], dims["conv"]
    grp = hf // hk
    w_in = P["w_in"][l]
    splits = np.cumsum([dims["q_rank"], lat, rope, hf * hd, hk * hd, hk * hd, hf, cw, cw, cw])
    zq, zkv, zkr, zfq, zfk, zfv, zff, zb, zc, zh, zg = jnp.split(w_in, splits, axis=1)
    misc = jnp.concatenate([_pad_cols(zkr, LANE), _pad_cols(_rot_half_cols(zkr), LANE), _pad_cols(zff, LANE)], axis=1)
    fq_scale = hd ** -0.5
    fq_pad = jnp.concatenate([_pad_cols(zfq[:, h * hd:(h + 1) * hd] * fq_scale, LANE) for h in range(hf)], axis=1)
    fk_pad = jnp.concatenate([_pad_cols(zfk[:, g * hd:(g + 1) * hd], LANE) for g in range(hk)], axis=1)
    fv_pad = jnp.concatenate([_pad_cols(zfv[:, g * hd:(g + 1) * hd], LANE) for g in range(hk)], axis=1)
    fkv = jnp.concatenate([_pad_cols(zfk, LANE), _pad_cols(zfv, LANE)], axis=1)
    w_big = jnp.concatenate([zq, zkv, misc, fq_pad, fkv, fk_pad, zb, zc, zh], axis=1).astype(BF)

    w_uq = P["w_uq"][l].reshape(dims["q_rank"], H, nope + rope)
    zpad = jnp.zeros((dims["q_rank"], LANE - nope - rope), F32)
    zn = jnp.zeros((dims["q_rank"], nope), F32)
    zr = jnp.zeros((dims["q_rank"], rope), F32)
    wa = jnp.concatenate([jnp.concatenate([w_uq[:, h, nope:], zpad, w_uq[:, h, :nope]], axis=1) for h in range(H)], axis=1)
    wb = jnp.concatenate([jnp.concatenate([_rot_half_cols(w_uq[:, h, nope:]), zpad, zn], axis=1) for h in range(H)], axis=1)
    del zr
    w_uk = P["w_uk"][l]
    w_uv = P["w_uv"][l]
    vdim = w_uv.shape[2]
    zl = jnp.zeros((lat, LANE - nope), F32)
    wk_pad = jnp.concatenate([jnp.concatenate([zl, w_uk[:, h, :]], axis=1) for h in range(H)], axis=1)
    wvt_pad = jnp.concatenate([_pad_cols(w_uv[:, h, :], LANE) for h in range(H)], axis=1).T
    wukt_pad = jnp.stack([jnp.concatenate([zl, w_uk[:, h, :]], axis=1).T for h in range(H)], axis=0)
    wuv_pad = jnp.stack([_pad_cols(w_uv[:, h, :], LANE) for h in range(H)], axis=0)

    def pad_rows(w, per, n):
        return jnp.concatenate([jnp.pad(w[h * per:(h + 1) * per], ((0, LANE - per), (0, 0))) for h in range(n)], axis=0)

    F = P["w_ff1_out"].shape[1]
    out = dict(
        seg=dims["seg"], heads=H, fox_heads=hf, rope=rope, conv=cw, kv_rank=lat,
        mla_scale=float((nope + rope) ** -0.5),
        g_mix=P["g_mix"][l][None, :], w_in=w_big, g_q=P["g_q"][l][None, :], g_kv=P["g_kv"][l][None, :],
        wuq_a=wa.astype(BF), wuq_b=wb.astype(BF), wk_pad=wk_pad.astype(BF), wvt_pad=wvt_pad.astype(BF),
        wfvt_pad=fv_pad.T.astype(BF),
        wukt_pad=wukt_pad.astype(BF), wuv_pad=wuv_pad.astype(BF),
        b_forget=_pad_cols(P["b_forget"][l][None, :], LANE), w_conv=P["w_conv"][l],
        w_gate=zg.astype(BF), b_gate=P["b_gate"][l][None, :],
        w_pa=pad_rows(P["w_pa"][l], vdim, H).astype(BF), w_pb=pad_rows(P["w_pb"][l], hd, hf).astype(BF),
        w_pc=P["w_pc"][l].astype(BF), w_o=P["w_o"][l].astype(BF),
        g_ffn1=P["g_ffn1"][l][None, :], g_ffn2=P["g_ffn2"][l][None, :],
        ff1=(P["w_ff1_in"][l][:, :F].astype(BF), P["w_ff1_in"][l][:, F:].astype(BF), P["w_ff1_out"][l].astype(BF)),
        ff2=(P["w_ff2_in"][l][:, :F].astype(BF), P["w_ff2_in"][l][:, F:].astype(BF), P["w_ff2_out"][l].astype(BF)),
    )
    return out


def _rope_tables(pos, rope):
    half = rope // 2
    inv = ROPE_THETA ** (-jnp.arange(half, dtype=F32) / half)
    ang = pos.astype(F32)[:, None] * inv[None, :]
    n = pos.shape[0]
    cos = jnp.concatenate([jnp.cos(ang), jnp.cos(ang), jnp.zeros((n, LANE // 2 - rope), F32),
                           jnp.ones((n, LANE // 2), F32)], axis=1)
    sin = jnp.concatenate([jnp.sin(ang), jnp.sin(ang), jnp.zeros((n, LANE - rope), F32)], axis=1)
    return dict(cos=cos, sin=sin)


def _decay_placement(hf, hk):
    grp = hf // hk
    hd = LANE // 2
    pq = np.zeros((3 * LANE, hf * LANE), np.float32)
    pk = np.zeros((3 * LANE, hk * LANE), np.float32)
    ones_row = hf
    for h in range(hf):
        g, j = divmod(h, grp)
        for part in range(3):
            pq[part * LANE + h, h * LANE + hd + part] = 1.0
            pq[ones_row, h * LANE + hd + 3 + 3 * j + part] = 1.0
            pk[part * LANE + h, g * LANE + hd + 3 + 3 * j + part] = -1.0
    for g in range(hk):
        for part in range(3):
            pk[ones_row, g * LANE + hd + part] = 1.0
    return jnp.asarray(pq, BF), jnp.asarray(pk, BF)


def _dims(P, cache_fox_k):
    L, D, _ = P["w_ada"].shape
    lat, H, nope = P["w_uk"].shape[1:]
    q_rank = P["g_q"].shape[1]
    rope = P["w_uq"].shape[2] // H - nope
    hf = P["b_forget"].shape[1]
    hk, hd = cache_fox_k.shape[3], cache_fox_k.shape[4]
    cw = P["w_conv"].shape[2]
    assert nope + rope <= LANE and nope == LANE // 2 and hd == LANE // 2 and hk == 2
    assert hd + 3 + 3 * (hf // hk) <= LANE and hf < LANE
    return dict(L=L, D=D, H=H, nope=nope, rope=rope, lat=lat, q_rank=q_rank, HF=hf, HK=hk, HD=hd, conv=cw,
                seg=_Seg(q_rank, lat, hf, cw))


def _layer_dense_pre(x, mod, W, tm):
    return _ffn_call(x, mod, 0, W["g_ffn1"], *W["ff1"], tm)


def _layer_dense_post(x, mod, W, oa, ob, oc, tm):
    x = _mix_out_call(x, mod, W, oa, ob, oc, tm)
    return _ffn_call(x, mod, 6, W["g_ffn2"], *W["ff2"], tm)


def kernel(x_prompt, x_sample, cache_mla_latent, cache_mla_krope, cache_fox_k, cache_fox_v, cache_fox_logf,
           state_conv, page_table, c_prompt, c_sample, w_ada, b_ada, g_ffn1, w_ff1_in, w_ff1_out, g_mix, w_in,
           g_q, w_uq, g_kv, w_uk, w_uv, b_forget, w_conv, b_gate, w_pa, w_pb, w_pc, w_o, g_ffn2, w_ff2_in,
           w_ff2_out, g_final):
    P = dict(w_ada=w_ada, b_ada=b_ada, g_ffn1=g_ffn1, w_ff1_in=w_ff1_in, w_ff1_out=w_ff1_out, g_mix=g_mix,
             w_in=w_in, g_q=g_q, w_uq=w_uq, g_kv=g_kv, w_uk=w_uk, w_uv=w_uv, b_forget=b_forget, w_conv=w_conv,
             b_gate=b_gate, w_pa=w_pa, w_pb=w_pb, w_pc=w_pc, w_o=w_o, g_ffn2=g_ffn2, w_ff2_in=w_ff2_in,
             w_ff2_out=w_ff2_out)
    dims = _dims(P, cache_fox_k)
    L, D, H, hf, hk, hd = dims["L"], dims["D"], dims["H"], dims["HF"], dims["HK"], dims["HD"]
    lat, rope, cw = dims["lat"], dims["rope"], dims["conv"]
    Bp, T, _ = x_prompt.shape
    B, Ts, _ = x_sample.shape
    assert Bp == 1 and Ts == SUBLANE
    n_pages, page = page_table.shape[1], cache_mla_latent.shape[2]
    past = n_pages * page
    Ns = B * Ts

    tm_p = min(512, T)
    tm_in = min(256, T)
    tm_s = min(512, Ns)
    tq = min(512, T)
    chunk = min(32, n_pages // 2)

    c_rows = jnp.concatenate([jnp.broadcast_to(c_prompt, (SUBLANE, D)), c_sample], axis=0)
    mod_all = _mod_call(c_rows, w_ada.astype(BF), b_ada)
    layers = [_pack_layer(l, P, dims) for l in range(L)]
    pq, pk = _decay_placement(hf, hk)
    tri = lambda n: jnp.asarray(np.tril(np.ones((n, n), np.float32)), BF)
    seq_tri = jnp.asarray(np.kron(np.eye(tm_s // Ts, dtype=np.float32), np.tril(np.ones((Ts, Ts), np.float32))), BF)

    tab_p = _rope_tables(jnp.arange(T, dtype=jnp.int32), rope)
    x = x_prompt
    p_state = []
    for l in range(L):
        W = layers[l]
        mod = mod_all[l, 0:1][:, None, :]
        x = _layer_dense_pre(x, mod, W, tm_p)
        (c_new, kr_new, fk, fv, logf, oconv, utail, qm, km, vm, qf, kf, vf) = _mix_in_call(
            x, mod, W, tab_p, True, tm_in, dict(ltri=tri(tm_in), pq=pq, pk=pk))
        o_mla = _flash_call(qm, km, vm, tq)
        o_fox = _flash_call(qf, kf, vf, tq)
        x = _layer_dense_post(x, mod, W, o_mla, o_fox, oconv, tm_p)
        p_state.append((c_new.reshape(1, T, lat), kr_new.reshape(1, T, rope),
                        fk.reshape(1, T, hk, hd), fv.reshape(1, T, hk, hd), logf.reshape(1, T, hf),
                        utail[-1, SUBLANE - 2:, :][None]))
    y_prompt = _final_call(x.reshape(T, D), g_final[None, :], tm_p).reshape(1, T, D)

    pt3 = page_table.reshape(B, 1, n_pages)
    caches = (cache_mla_latent, jnp.transpose(cache_mla_krope, (0, 1, 3, 2)),
              jnp.transpose(cache_fox_k, (0, 1, 3, 4, 2)), jnp.transpose(cache_fox_v, (0, 1, 3, 4, 2)),
              jnp.transpose(cache_fox_logf, (0, 1, 3, 2)))
    tab_s = _rope_tables(past + (jnp.arange(Ns, dtype=jnp.int32) % Ts), rope)
    x = x_sample
    s_state = []
    for l in range(L):
        W = layers[l]
        mod = mod_all[l, SUBLANE:][:, None, :]
        x = _layer_dense_pre(x, mod, W, tm_s)
        st = state_conv[l]
        zero = jnp.zeros((B, Ts - 2, cw), F32)
        halo_a = jnp.concatenate([st, zero], axis=1).reshape(Ns, cw)
        halo_b = jnp.concatenate([st[:, 1:2], jnp.zeros((B, Ts - 1, cw), F32)], axis=1).reshape(Ns, cw)
        (c_new, kr_new, fk, fv, logf, oconv, u_all, qm, qlat, qf, cum) = _mix_in_call(
            x, mod, W, tab_s, False, tm_s, dict(ltri=seq_tri, halo_a=halo_a, halo_b=halo_b))

        def head_major(a):
            w = a.shape[1] // H
            return a.reshape(B, Ts, H, w).transpose(0, 2, 1, 3).reshape(B, H * Ts, w)

        fq_hm = head_major(qf)[:, :, :hd].reshape(B, hf, Ts, hd)
        zq_ = jnp.zeros_like(fq_hm[:, :hf // hk])
        fq_dec = jnp.concatenate([jnp.concatenate([fq_hm[:, :hf // hk], zq_], axis=-1),
                                  jnp.concatenate([zq_, fq_hm[:, hf // hk:]], axis=-1)], axis=1)
        cum_new = cum[:, :hf].reshape(B, Ts, hf)
        cum_hm = cum_new.transpose(0, 2, 1)
        pad_keys = lambda a: jnp.pad(a.reshape(B, Ts, -1).astype(BF), ((0, 0), (0, page - Ts), (0, 0)))
        dq = dict(
            qlat=head_major(qlat), qrope=head_major(qm)[:, :, :rope],
            fq=fq_dec.reshape(B, hf * Ts, LANE),
            cumb=jnp.broadcast_to(cum_hm.reshape(B, hf * Ts, 1), (B, hf * Ts, LANE)),
            cn=pad_keys(c_new), krn=pad_keys(kr_new), fkn=pad_keys(fk), fvn=pad_keys(fv),
            rtn=jnp.pad(-cum_hm, ((0, 0), (0, 0), (0, page - Ts))),
        )
        o_mla, o_fox = _decode_call(l, pt3, dq, caches, W["wuv_pad"], chunk)
        x = _layer_dense_post(x, mod, W, o_mla.reshape(Ns, H * LANE), o_fox.reshape(Ns, hf * LANE), oconv, tm_s)
        s_state.append((c_new.reshape(B, Ts, lat), kr_new.reshape(B, Ts, rope),
                        fk.reshape(B, Ts, hk, hd), fv.reshape(B, Ts, hk, hd), logf.reshape(B, Ts, hf),
                        u_all.reshape(B, Ts, cw)[:, Ts - 2:, :]))
    y_sample = _final_call(x.reshape(Ns, D), g_final[None, :], tm_s).reshape(B, Ts, D)

    stack = lambda states, i: jnp.stack([s[i] for s in states], axis=0)
    return ((y_prompt, y_sample) + tuple(stack(p_state, i) for i in range(6))
            + tuple(stack(s_state, i) for i in range(6)))
```
